```python
import math
import jax
import jax.numpy as jnp
from jax import lax
import numpy as np

D_MODEL = 1024
BATCH = 8
SEQ = 4096
DEPTH = 2
DEC_BATCH = 8
DEC_SEQ = 2048
PAST_LEN = 128

H_A = 4
DK_A = 64
DV_A = 128
Q_BLOCK = 128
H_B = 8
KV_B = 2
GQ_B = 4
DH_B = 64
WINDOW = 128
BLOCK_B = 128
H_C = 4
DK_C = 64
DV_C = 128
GATE_RANK = 16
GATE_TAU = 16
CHUNK = 64
BRANCH_W = 512
N_BRANCH = 3
N_BUCKETS = 32
MAX_DIST = 128
N_EXPERTS = 16
EC_FACTOR = 2
D_FF = 2048
P_DIM = 256
NORM_EPS = 1e-6

IN_WIDTHS = (H_A * 2 * DK_A, H_A * 2 * DK_A, H_A * DV_A,
             H_B * DH_B, KV_B * DH_B, KV_B * DH_B,
             H_C * DK_C, H_C * DK_C, H_C * DV_C, 2 * GATE_RANK, H_C * DV_C)
D_IN = (H_A * 2 * DK_A * 2 + H_A * DV_A + H_B * DH_B + 2 * KV_B * DH_B
        + 2 * H_C * DK_C + H_C * DV_C + 2 * GATE_RANK + H_C * DV_C)

kernel_name = "hybrid_diffattn_swa_gla_ec_encoder"


def rmsnorm(x, g):
    xf = x.astype(jnp.float32)
    r = lax.rsqrt(jnp.mean(xf * xf, axis=-1, keepdims=True) + NORM_EPS)
    return (xf * r).astype(x.dtype) * g


def t5_bucket(rel):
    half = N_BUCKETS // 2
    max_exact = half // 2
    ret = jnp.where(rel > 0, half, 0)
    n = jnp.abs(rel)
    nf = jnp.maximum(n, 1).astype(jnp.float32)
    large = max_exact + (jnp.log(nf / max_exact) / math.log(MAX_DIST / max_exact)
                         * (half - max_exact)).astype(jnp.int32)
    large = jnp.minimum(large, half - 1)
    return ret + jnp.where(n < max_exact, n, large)


def diff_attention(q, k, v, rel_table, lam, lam_init, sub_g):
    B_, S_ = q.shape[:2]
    nqb = S_ // Q_BLOCK
    qb = (q * DK_A ** -0.5).reshape(B_, nqb, Q_BLOCK, H_A, 2, DK_A).transpose(1, 0, 2, 3, 4, 5)
    kpos = jnp.arange(S_)

    def block(args):
        qblk, i = args
        s = jnp.einsum('bqhmd,bkhmd->bhmqk', qblk, k).astype(jnp.float32)
        qpos = i * Q_BLOCK + jnp.arange(Q_BLOCK)
        bias = rel_table[t5_bucket(kpos[None, :] - qpos[:, None])]
        s = s + bias.transpose(2, 0, 1)[None, :, None].astype(jnp.float32)
        pr = jax.nn.softmax(s, axis=-1)
        w = pr[:, :, 0] - lam * pr[:, :, 1]
        return jnp.einsum('bhqk,bkhd->bqhd', w.astype(v.dtype), v)

    o = lax.map(block, (qb, jnp.arange(nqb)))
    o = o.transpose(1, 0, 2, 3, 4).reshape(B_, S_, H_A, DV_A)
    o = rmsnorm(o, sub_g) * (1.0 - lam_init)
    return o.reshape(B_, S_, H_A * DV_A)


def window_gqa(q, k, v, rel_table, sink):
    B_, S_ = q.shape[:2]
    nb = S_ // BLOCK_B
    qb = (q * DH_B ** -0.5).reshape(B_, nb, BLOCK_B, KV_B, GQ_B, DH_B)

    def band(t):
        tp = jnp.pad(t, ((0, 0), (BLOCK_B, BLOCK_B), (0, 0), (0, 0)))
        tp = tp.reshape(B_, nb + 2, BLOCK_B, KV_B, DH_B)
        return jnp.concatenate([tp[:, :-2], tp[:, 1:-1], tp[:, 2:]], axis=2)

    kb, vb = band(k), band(v)
    s = jnp.einsum('bnqcgd,bnjcd->bncgqj', qb, kb).astype(jnp.float32)
    rel = jnp.arange(3 * BLOCK_B)[None, :] - BLOCK_B - jnp.arange(BLOCK_B)[:, None]
    bias = rel_table[t5_bucket(rel)].transpose(2, 0, 1).reshape(KV_B, GQ_B, BLOCK_B, 3 * BLOCK_B)
    kabs = (jnp.arange(nb)[:, None] - 1) * BLOCK_B + jnp.arange(3 * BLOCK_B)[None, :]
    valid = (jnp.abs(rel) <= WINDOW)[None] & ((kabs >= 0) & (kabs < S_))[:, None, :]
    s = jnp.where(valid[None, :, None, None], s + bias.astype(jnp.float32), -jnp.inf)
    sk = sink.reshape(KV_B, GQ_B)[None, None, :, :, None, None].astype(jnp.float32)
    m = jnp.maximum(jnp.max(s, axis=-1, keepdims=True), sk)
    pr = jnp.exp(s - m)
    pr = pr / (jnp.sum(pr, axis=-1, keepdims=True) + jnp.exp(sk - m))
    o = jnp.einsum('bncgqj,bnjcd->bnqcgd', pr.astype(v.dtype), vb)
    return o.reshape(B_, S_, H_B * DH_B)


def gla_scan(q, k, v, g):
    B_, H_, S_, _ = q.shape
    nc = S_ // CHUNK
    q, k, v, g = (t.reshape(B_, H_, nc, CHUNK, t.shape[-1]) for t in (q, k, v, g))
    b = jnp.cumsum(g, axis=3)
    b_last = b[:, :, :, -1:]
    qd = q * jnp.exp(b)
    kd = k * jnp.exp(-b)
    ki = k * jnp.exp(b_last - b)
    lower = jnp.tril(jnp.ones((CHUNK, CHUNK), dtype=bool))
    a = jnp.where(lower, jnp.einsum('bhnid,bhnjd->bhnij', qd, kd), 0.0)
    o_intra = jnp.einsum('bhnij,bhnjv->bhniv', a, v)
    kv = jnp.einsum('bhnjd,bhnjv->nbhdv', ki, v)
    decay = jnp.exp(b_last[:, :, :, 0]).transpose(2, 0, 1, 3)

    def step(state, inp):
        dec, kvc = inp
        return dec[..., None] * state + kvc, state

    _, s_prev = lax.scan(step, jnp.zeros((B_, H_, DK_C, DV_C), jnp.float32), (decay, kv))
    o_inter = jnp.einsum('bhnid,nbhdv->bhniv', qd, s_prev)
    return (o_intra + o_inter).reshape(B_, H_, S_, DV_C)


def gla_bidir(q, k, v, a_lr, r, w_up, b_up, norm_g):
    f32 = jnp.float32
    B_, S_ = q.shape[:2]
    g = jax.nn.log_sigmoid((jnp.einsum('bsdr,drk->bsdk', a_lr, w_up) + b_up).astype(f32)) / GATE_TAU
    heads = lambda t: t.astype(f32).transpose(0, 2, 1, 3)
    qh = heads(q) * DK_C ** -0.5
    kh, vh = heads(k), heads(v)
    gf = heads(g[:, :, 0].reshape(B_, S_, H_C, DK_C))
    gb = heads(g[:, :, 1].reshape(B_, S_, H_C, DK_C))
    rev = lambda t: jnp.flip(t, axis=2)
    o = gla_scan(qh, kh, vh, gf) + rev(gla_scan(rev(qh), rev(kh), rev(vh), rev(gb)))
    o = o.transpose(0, 2, 1, 3)
    o = rmsnorm(o, norm_g.astype(f32)) * jax.nn.silu(r.astype(f32)).reshape(B_, S_, H_C, DV_C)
    return o.reshape(B_, S_, H_C * DV_C).astype(r.dtype)


def expert_choice_moe(h, w_router, b_router, w_gate, w_up, w_down):
    B_, S_, D_ = h.shape
    n = B_ * S_
    cap = EC_FACTOR * n // N_EXPERTS
    xt = h.reshape(n, D_)
    aff = jax.nn.softmax((xt @ w_router).astype(jnp.float32) + b_router.astype(jnp.float32), axis=-1)
    gates, idx = lax.top_k(aff.T, cap)
    xe = xt[idx]
    hid = jax.nn.silu(jnp.einsum('ecd,edf->ecf', xe, w_gate)) * jnp.einsum('ecd,edf->ecf', xe, w_up)
    ye = jnp.einsum('ecf,efd->ecd', hid, w_down) * gates[..., None].astype(h.dtype)
    y = jnp.zeros_like(xt).at[idx.reshape(-1)].add(ye.reshape(-1, D_))
    return y.reshape(B_, S_, D_)


def trunk(x, pe, rel_bias, g_mix, w_in, lam_a, subln_a, sink_b, w_alpha_up, b_alpha_up, norm_c,
          w_branch, w_merge_gate, w_out, g_ffn, w_router, b_router, w_exp_gate, w_exp_up, w_exp_down,
          w_pe_proj, w_pe_gate, g_final):
    B_, S_, D_ = x.shape
    splits = np.cumsum(IN_WIDTHS)[:-1].tolist()
    bias_a = rel_bias[:, :H_A]
    bias_b = rel_bias[:, H_A:]
    for l in range(DEPTH):
        h = rmsnorm(x, g_mix[l])
        z = h @ w_in[l]
        qa, ka, va, qb, kb, vb, qc, kc, vc, ac, rc = jnp.split(z, splits, axis=-1)
        lam_init = 0.8 - 0.6 * math.exp(-0.3 * l)
        lp = lam_a[l]
        lam = (jnp.exp(jnp.sum(lp[0] * lp[1])) - jnp.exp(jnp.sum(lp[2] * lp[3]))).astype(jnp.float32) + lam_init
        oa = diff_attention(qa.reshape(B_, S_, H_A, 2, DK_A), ka.reshape(B_, S_, H_A, 2, DK_A),
                            va.reshape(B_, S_, H_A, DV_A), bias_a, lam, lam_init, subln_a[l])
        ob = window_gqa(qb.reshape(B_, S_, KV_B, GQ_B, DH_B), kb.reshape(B_, S_, KV_B, DH_B),
                        vb.reshape(B_, S_, KV_B, DH_B), bias_b, sink_b[l])
        oc = gla_bidir(qc.reshape(B_, S_, H_C, DK_C), kc.reshape(B_, S_, H_C, DK_C),
                       vc.reshape(B_, S_, H_C, DV_C), ac.reshape(B_, S_, 2, GATE_RANK), rc,
                       w_alpha_up[l], b_alpha_up[l], norm_c[l])
        branches = jnp.einsum('bsgw,gwd->bsgd', jnp.stack([oa, ob, oc], axis=2), w_branch[l])
        gates = jax.nn.sigmoid(h @ w_merge_gate[l]).reshape(B_, S_, N_BRANCH, D_)
        x = x + jnp.sum(gates * branches, axis=2) @ w_out[l]
        x = x + expert_choice_moe(rmsnorm(x, g_ffn[l]), w_router[l], b_router[l],
                                  w_exp_gate[l], w_exp_up[l], w_exp_down[l])
        x = x + jax.nn.sigmoid(x @ w_pe_gate[l]) * (pe[l] @ w_pe_proj[l])
    return rmsnorm(x, g_final)


def setup_inputs(seed: int = 0) -> dict:
    key = jax.random.key(seed)
    ks = jax.random.split(key, 26)
    nrm = lambda k, shape, scale: jax.random.normal(k, shape, jnp.float32) * scale
    D = D_MODEL
    return {
        "x_prompt": nrm(ks[0], (BATCH, SEQ, D), 1.0),
        "x_sample": nrm(ks[1], (DEC_BATCH, DEC_SEQ, D), 1.0),
        "p_prompt": nrm(ks[2], (DEPTH, BATCH, SEQ, P_DIM), 1.0),
        "p_sample": nrm(ks[3], (DEPTH, DEC_BATCH, DEC_SEQ, P_DIM), 1.0),
        "rel_bias": nrm(ks[4], (N_BUCKETS, H_A + H_B), 0.5),
        "g_mix": 1.0 + nrm(ks[5], (DEPTH, D), 0.02),
        "w_in": nrm(ks[6], (DEPTH, D, D_IN), D ** -0.5),
        "lam_a": nrm(ks[7], (DEPTH, 4, DK_A), 0.1),
        "subln_a": 1.0 + nrm(ks[8], (DEPTH, DV_A), 0.02),
        "sink_b": nrm(ks[9], (DEPTH, H_B), 0.5),
        "w_alpha_up": nrm(ks[10], (DEPTH, 2, GATE_RANK, H_C * DK_C), GATE_RANK ** -0.5),
        "b_alpha_up": nrm(ks[11], (DEPTH, 2, H_C * DK_C), 0.1),
        "norm_c": 1.0 + nrm(ks[12], (DEPTH, H_C, DV_C), 0.02),
        "w_branch": nrm(ks[13], (DEPTH, N_BRANCH, BRANCH_W, D), BRANCH_W ** -0.5),
        "w_merge_gate": nrm(ks[14], (DEPTH, D, N_BRANCH * D), D ** -0.5),
        "w_out": nrm(ks[15], (DEPTH, D, D), D ** -0.5),
        "g_ffn": 1.0 + nrm(ks[16], (DEPTH, D), 0.02),
        "w_router": nrm(ks[17], (DEPTH, D, N_EXPERTS), D ** -0.5),
        "b_router": nrm(ks[18], (DEPTH, N_EXPERTS), 0.01),
        "w_exp_gate": nrm(ks[19], (DEPTH, N_EXPERTS, D, D_FF), D ** -0.5),
        "w_exp_up": nrm(ks[20], (DEPTH, N_EXPERTS, D, D_FF), D ** -0.5),
        "w_exp_down": nrm(ks[21], (DEPTH, N_EXPERTS, D_FF, D), D_FF ** -0.5),
        "w_pe_proj": nrm(ks[22], (DEPTH, P_DIM, D), P_DIM ** -0.5),
        "w_pe_gate": nrm(ks[23], (DEPTH, D, D), D ** -0.5),
        "g_final": 1.0 + nrm(ks[24], (D,), 0.02),
    }


def reference(x_prompt, x_sample, p_prompt, p_sample, rel_bias, g_mix, w_in, lam_a, subln_a, sink_b,
              w_alpha_up, b_alpha_up, norm_c, w_branch, w_merge_gate, w_out, g_ffn, w_router, b_router,
              w_exp_gate, w_exp_up, w_exp_down, w_pe_proj, w_pe_gate, g_final):
    weights = (rel_bias, g_mix, w_in, lam_a, subln_a, sink_b, w_alpha_up, b_alpha_up, norm_c,
               w_branch, w_merge_gate, w_out, g_ffn, w_router, b_router, w_exp_gate, w_exp_up,
               w_exp_down, w_pe_proj, w_pe_gate, g_final)
    y_prompt = trunk(x_prompt, p_prompt, *weights)
    y_sample = trunk(x_sample, p_sample, *weights)
    return (y_prompt, y_sample)
```

```python
import functools
import math

import jax
import jax.numpy as jnp
from jax import lax
from jax.experimental import pallas as pl
from jax.experimental.pallas import tpu as pltpu

F32 = jnp.float32
BF16 = jnp.bfloat16

D_MODEL = 1024
DEPTH = 2
H_A, DK_A, DV_A = 4, 64, 128
H_B, KV_B, GQ_B, DH_B = 8, 2, 4, 64
WINDOW, BLOCK_B = 128, 128
H_C, DK_C, DV_C = 4, 64, 128
GATE_RANK, GATE_TAU, CHUNK = 16, 16, 64
BRANCH_W, N_BRANCH = 512, 3
N_BUCKETS, MAX_DIST = 32, 128
N_EXPERTS, EC_FACTOR, D_FF = 16, 2, 2048
P_DIM = 256
NORM_EPS = 1e-6
LANES = 128
VMEM_LIMIT = 56 * 1024 * 1024

_IN_COLS = (
    ("qa", 512, BF16, DK_A ** -0.5), ("ka", 512, BF16, 1.0), ("va", 512, BF16, 1.0),
    ("qb", 1024, BF16, DH_B ** -0.5), ("kvb", 256, BF16, 1.0),
    ("qc", 512, F32, DK_C ** -0.5), ("kc", 512, F32, 1.0), ("vc", 512, BF16, 1.0),
    ("ac", 128, F32, 1.0), ("rc", 512, BF16, 1.0),
)


def _cparams(n_axes):
    return pltpu.CompilerParams(dimension_semantics=("arbitrary",) * n_axes,
                                vmem_limit_bytes=VMEM_LIMIT)


def _const_spec(shape):
    nd = len(shape)
    return pl.BlockSpec(shape, lambda *_: (0,) * nd)


def _pick(n, pref):
    t = min(n, pref)
    while n % t:
        t //= 2
    return t


def _rms(x):
    return x * lax.rsqrt(jnp.mean(x * x, axis=-1, keepdims=True) + NORM_EPS)


def _t5_bucket(rel):
    half = N_BUCKETS // 2
    max_exact = half // 2
    ret = jnp.where(rel > 0, half, 0)
    n = jnp.abs(rel)
    nf = jnp.maximum(n, 1).astype(F32)
    large = max_exact + (jnp.log(nf / max_exact) / math.log(MAX_DIST / max_exact)
                         * (half - max_exact)).astype(jnp.int32)
    large = jnp.minimum(large, half - 1)
    return ret + jnp.where(n < max_exact, n, large)


def _inproj_kernel(x_ref, g_ref, w_ref, *out_refs):
    h = (_rms(x_ref[...]) * g_ref[...]).astype(BF16)
    c0 = 0
    for o_ref, (_, width, _, scale) in zip(out_refs, _IN_COLS):
        acc = jnp.dot(h, w_ref[:, c0:c0 + width], preferred_element_type=F32)
        if scale != 1.0:
            acc = acc * scale
        o_ref[...] = acc.astype(o_ref.dtype)
        c0 += width


def _inproj(x2d, g, w_cat):
    n, d = x2d.shape
    tm = _pick(n, 512)
    out_shape = [jax.ShapeDtypeStruct((n, w), dt) for _, w, dt, _ in _IN_COLS]
    out_specs = [pl.BlockSpec((tm, w), lambda i: (i, 0)) for _, w, _, _ in _IN_COLS]
    return pl.pallas_call(
        _inproj_kernel,
        out_shape=out_shape,
        grid=(n // tm,),
        in_specs=[pl.BlockSpec((tm, d), lambda i: (i, 0)), _const_spec(g.shape), _const_spec(w_cat.shape)],
        out_specs=out_specs,
        compiler_params=_cparams(1),
        name="inproj",
    )(x2d, g, w_cat)


def _diffattn_kernel(lam_ref, subg_ref, cst_ref, band_ref, q_ref, k_ref, v_ref, o_ref, s_ref, *, lam_init):
    i = pl.program_id(2)
    nq = pl.num_programs(2)
    tq = q_ref.shape[0]
    s_len = k_ref.shape[0]
    lp = lam_ref[...]
    lam = (jnp.exp(jnp.sum(lp[0:1] * lp[1:2], axis=-1, keepdims=True))
           - jnp.exp(jnp.sum(lp[2:3] * lp[3:4], axis=-1, keepdims=True))) + lam_init
    q = q_ref[...]
    k = k_ref[...]
    lane = lax.broadcasted_iota(jnp.int32, (1, LANES), 1)
    zero = jnp.zeros_like(q)
    col = lax.broadcasted_iota(jnp.int32, (tq, s_len), 1)
    base = jnp.where(col < i * tq, cst_ref[0:1, 0:1], cst_ref[1:2, 0:1])
    for m in range(2):
        qm = jnp.where((lane < DK_A) == (m == 0), q, zero)
        s_ref[m] = lax.dot_general(qm, k, (((1,), (1,)), ((), ())), preferred_element_type=F32) + base
    for d in (-1, 0, 1):
        kb = i + d

        @pl.when((kb >= 0) & (kb < nq))
        def _(d=d, kb=kb):
            off = pl.multiple_of(kb * tq, tq)
            delta = band_ref[d + 1]
            for m in range(2):
                s_ref[m, :, pl.ds(off, tq)] += delta

    probs = []
    for m in range(2):
        s = s_ref[m]
        p = jnp.exp(s - jnp.max(s, axis=-1, keepdims=True))
        probs.append(p * (1.0 / jnp.sum(p, axis=-1, keepdims=True)))
    w = (probs[0] - lam * probs[1]).astype(BF16)
    o = jnp.dot(w, v_ref[...], preferred_element_type=F32)
    o_ref[...] = (_rms(o) * subg_ref[...] * (1.0 - lam_init)).astype(o_ref.dtype)


def _diffattn(qa, ka, va, lam_l, subg_l, bias_a, lam_init):
    bsz, s_len, _ = qa.shape
    tq = _pick(s_len, 256)
    nq = s_len // tq
    ii = jnp.arange(tq)
    rel = jnp.stack([d * tq + ii[None, :] - ii[:, None] for d in (-1, 0, 1)])
    table = bias_a[_t5_bucket(rel)]
    c_left = bias_a[N_BUCKETS // 2 - 1]
    c_right = bias_a[N_BUCKETS - 1]
    side = jnp.stack([c_left, c_right, c_right])
    band = (table - side[:, None, None, :]).transpose(3, 0, 1, 2)
    cst = jnp.broadcast_to(jnp.stack([c_left, c_right], axis=1)[:, :, None], (H_A, 2, LANES))
    cst = jnp.pad(cst, ((0, 0), (0, 6), (0, 0)))
    return pl.pallas_call(
        functools.partial(_diffattn_kernel, lam_init=lam_init),
        out_shape=jax.ShapeDtypeStruct((bsz, s_len, H_A * DV_A), BF16),
        grid=(bsz, H_A, nq),
        in_specs=[
            _const_spec(lam_l.shape),
            _const_spec(subg_l.shape),
            pl.BlockSpec((None, 8, LANES), lambda b, h, i: (h, 0, 0)),
            pl.BlockSpec((None, 3, tq, tq), lambda b, h, i: (h, 0, 0, 0)),
            pl.BlockSpec((None, tq, LANES), lambda b, h, i: (b, i, h)),
            pl.BlockSpec((None, s_len, LANES), lambda b, h, i: (b, 0, h)),
            pl.BlockSpec((None, s_len, LANES), lambda b, h, i: (b, 0, h)),
        ],
        out_specs=pl.BlockSpec((None, tq, LANES), lambda b, h, i: (b, i, h)),
        scratch_shapes=[pltpu.VMEM((2, tq, s_len), F32)],
        compiler_params=_cparams(3),
        name="diffattn",
    )(lam_l, subg_l, cst, band, qa, ka, va)


def _winattn_kernel(sink_ref, band_ref, q_ref, kp_ref, kc_ref, kn_ref, o_ref, *, s_len):
    i = pl.program_id(1)
    kv = jnp.concatenate([kp_ref[...], kc_ref[...], kn_ref[...]], axis=0)
    kk = kv[:, :LANES]
    vv = kv[:, LANES:]
    qn = BLOCK_B
    jj = lax.broadcasted_iota(jnp.int32, (qn, 3 * qn), 1)
    qi = lax.broadcasted_iota(jnp.int32, (qn, 3 * qn), 0)
    rel = jj - qn - qi
    kabs = (i - 1) * qn + jj
    valid = (jnp.abs(rel) <= WINDOW) & (kabs >= 0) & (kabs < s_len)
    lane = lax.broadcasted_iota(jnp.int32, (1, LANES), 1)
    for g in range(GQ_B):
        outs = []
        for c in range(KV_B):
            h = c * GQ_B + g
            s = lax.dot_general(q_ref[:, h * LANES:(h + 1) * LANES], kk, (((1,), (1,)), ((), ())),
                                preferred_element_type=F32)
            s = jnp.where(valid, s + band_ref[h], -jnp.inf)
            sk = sink_ref[h:h + 1, 0:1]
            m = jnp.maximum(jnp.max(s, axis=-1, keepdims=True), sk)
            p = jnp.exp(s - m)
            p = p / (jnp.sum(p, axis=-1, keepdims=True) + jnp.exp(sk - m))
            outs.append(jnp.dot(p.astype(BF16), vv, preferred_element_type=F32))
        o_ref[:, g * LANES:(g + 1) * LANES] = jnp.where(lane < DH_B, outs[0], outs[1]).astype(o_ref.dtype)


def _winattn(qb, kvb, sink_l, bias_b):
    bsz, s_len, _ = qb.shape
    nb = s_len // BLOCK_B
    rel = jnp.arange(3 * BLOCK_B)[None, :] - BLOCK_B - jnp.arange(BLOCK_B)[:, None]
    band = bias_b[_t5_bucket(rel)].transpose(2, 0, 1)
    sink = jnp.broadcast_to(sink_l[:, None], (H_B, LANES))
    kvw = 2 * KV_B * DH_B
    return pl.pallas_call(
        functools.partial(_winattn_kernel, s_len=s_len),
        out_shape=jax.ShapeDtypeStruct((bsz, s_len, H_B * DH_B), BF16),
        grid=(bsz, nb),
        in_specs=[
            _const_spec(sink.shape),
            _const_spec(band.shape),
            pl.BlockSpec((None, BLOCK_B, H_B * LANES), lambda b, i: (b, i, 0)),
            pl.BlockSpec((None, BLOCK_B, kvw), lambda b, i: (b, jnp.maximum(i - 1, 0), 0)),
            pl.BlockSpec((None, BLOCK_B, kvw), lambda b, i: (b, i, 0)),
            pl.BlockSpec((None, BLOCK_B, kvw), lambda b, i: (b, jnp.minimum(i + 1, nb - 1), 0)),
        ],
        out_specs=pl.BlockSpec((None, BLOCK_B, H_B * DH_B), lambda b, i: (b, i, 0)),
        compiler_params=_cparams(2),
        name="winattn",
    )(sink, band, qb, kvb, kvb, kvb)


def _gla_kernel(q_ref, k_ref, v_ref, a_ref, r_ref, wbd_ref, bup_ref, ng_ref, o_ref, g_scr, acc_scr):
    s_len = q_ref.shape[0]
    nc = s_len // CHUNK
    pre = jnp.dot(a_ref[...].astype(BF16), wbd_ref[...], preferred_element_type=F32) + bup_ref[...]
    g_scr[...] = jax.nn.log_sigmoid(pre) * (1.0 / GATE_TAU)
    lane = lax.broadcasted_iota(jnp.int32, (1, LANES), 1)
    row = lax.broadcasted_iota(jnp.int32, (CHUNK, CHUNK), 0)
    colc = lax.broadcasted_iota(jnp.int32, (CHUNK, CHUNK), 1)

    def run(fwd):
        tri = (colc <= row) if fwd else (colc >= row)
        cum = tri.astype(F32)
        mask = (lane < DK_C) if fwd else (lane >= DK_C)

        def chunk(t, state):
            c = t if fwd else nc - 1 - t
            off = pl.multiple_of(c * CHUNK, CHUNK)
            g = g_scr[pl.ds(off, CHUNK), :]
            b = jnp.dot(cum, g, preferred_element_type=F32, precision=lax.Precision.HIGHEST)
            btot = b[CHUNK - 1:CHUNK] if fwd else b[0:1]
            q = q_ref[pl.ds(off, CHUNK), :]
            k = k_ref[pl.ds(off, CHUNK), :]
            v = v_ref[pl.ds(off, CHUNK), :]
            qd = jnp.where(mask, q * jnp.exp(b), 0.0).astype(BF16)
            kd = jnp.where(mask, k * jnp.exp(-b), 0.0).astype(BF16)
            ki = jnp.where(mask, k * jnp.exp(btot - b), 0.0).astype(BF16)
            a = lax.dot_general(qd, kd, (((1,), (1,)), ((), ())), preferred_element_type=F32)
            a = jnp.where(tri, a, 0.0).astype(BF16)
            o = jnp.dot(a, v, preferred_element_type=F32)
            o = o + lax.dot_general(qd, state.astype(BF16), (((1,), (1,)), ((), ())),
                                    preferred_element_type=F32)
            kv_t = lax.dot_general(v, ki, (((0,), (0,)), ((), ())), preferred_element_type=F32)
            if fwd:
                acc_scr[pl.ds(off, CHUNK), :] = o
            else:
                acc_scr[pl.ds(off, CHUNK), :] += o
            return jnp.exp(btot) * state + kv_t

        lax.fori_loop(0, nc, chunk, jnp.zeros((DV_C, LANES), F32))

    run(True)
    run(False)
    r = r_ref[...].astype(F32)
    o_ref[...] = (_rms(acc_scr[...]) * ng_ref[...] * (r * jax.nn.sigmoid(r))).astype(o_ref.dtype)


def _gla(qc, kc, vc, ac, rc, wbd, bup, ng):
    bsz, s_len, _ = qc.shape
    blk = lambda: pl.BlockSpec((None, s_len, LANES), lambda b, h: (b, 0, h))
    per_head = lambda r: pl.BlockSpec((None, r, LANES), lambda b, h: (h, 0, 0))
    return pl.pallas_call(
        _gla_kernel,
        out_shape=jax.ShapeDtypeStruct((bsz, s_len, H_C * DV_C), BF16),
        grid=(bsz, H_C),
        in_specs=[blk(), blk(), blk(),
                  pl.BlockSpec((None, s_len, LANES), lambda b, h: (b, 0, 0)),
                  blk(), per_head(LANES), per_head(1), per_head(1)],
        out_specs=blk(),
        scratch_shapes=[pltpu.VMEM((s_len, LANES), F32), pltpu.VMEM((s_len, LANES), F32)],
        compiler_params=_cparams(2),
        name="gla",
    )(qc, kc, vc, ac, rc, wbd, bup, ng)


def _merge_kernel(x_ref, oa_ref, ob_ref, oc_ref, gmix_ref, wmg_ref, wbr_ref, wout_ref, gffn_ref, wr_ref, br_ref,
                  x1_ref, xt_ref, aff_ref):
    x = x_ref[...]
    h = (_rms(x) * gmix_ref[...]).astype(BF16)
    acc = None
    for g, o_ref in enumerate((oa_ref, ob_ref, oc_ref)):
        gate = jax.nn.sigmoid(jnp.dot(h, wmg_ref[:, g * D_MODEL:(g + 1) * D_MODEL], preferred_element_type=F32))
        br = jnp.dot(o_ref[...], wbr_ref[g], preferred_element_type=F32)
        acc = gate * br if acc is None else acc + gate * br
    x1 = x + jnp.dot(acc.astype(BF16), wout_ref[...], preferred_element_type=F32)
    x1_ref[...] = x1
    xt = _rms(x1) * gffn_ref[...]
    xt_ref[...] = xt
    logits = jnp.dot(xt, wr_ref[...], preferred_element_type=F32, precision=lax.Precision.HIGHEST) + br_ref[...]
    lane = lax.broadcasted_iota(jnp.int32, logits.shape, 1)
    logits = jnp.where(lane < N_EXPERTS, logits, -jnp.inf)
    e = jnp.exp(logits - jnp.max(logits, axis=-1, keepdims=True))
    aff_ref[...] = e / jnp.sum(e, axis=-1, keepdims=True)


def _merge(x2d, oa, ob, oc, gmix, wmg, wbr, wout, gffn, wr, br):
    n, d = x2d.shape
    tm = _pick(n, 256)
    row = lambda w: pl.BlockSpec((tm, w), lambda i: (i, 0))
    return pl.pallas_call(
        _merge_kernel,
        out_shape=[jax.ShapeDtypeStruct((n, d), F32), jax.ShapeDtypeStruct((n, d), F32),
                   jax.ShapeDtypeStruct((n, LANES), F32)],
        grid=(n // tm,),
        in_specs=[row(d), row(BRANCH_W), row(BRANCH_W), row(BRANCH_W)]
        + [_const_spec(a.shape) for a in (gmix, wmg, wbr, wout, gffn, wr, br)],
        out_specs=[row(d), row(d), row(LANES)],
        compiler_params=_cparams(1),
        name="merge",
    )(x2d, oa, ob, oc, gmix, wmg, wbr, wout, gffn, wr, br)


def _prefix_counts(m):
    r = m.shape[0]
    li = lax.broadcasted_iota(jnp.int32, (LANES, LANES), 0)
    lj = lax.broadcasted_iota(jnp.int32, (LANES, LANES), 1)
    within = jnp.dot(m.astype(BF16), (li <= lj).astype(BF16), preferred_element_type=F32)
    rowtot = jnp.broadcast_to(within[:, LANES - 1:LANES], (r, LANES))
    ri = lax.broadcasted_iota(jnp.int32, (r, r), 0)
    rj = lax.broadcasted_iota(jnp.int32, (r, r), 1)
    before = jnp.dot((rj < ri).astype(F32), rowtot, preferred_element_type=F32, precision=lax.Precision.HIGHEST)
    return within, rowtot, before


def _select_kernel(aff_ref, idx_ref, gate_ref, *, cap):
    aff = aff_ref[...]
    r = aff.shape[0]
    bits = pltpu.bitcast(aff, jnp.int32)

    def bit_step(t, thr):
        cand = thr | jnp.left_shift(jnp.int32(1), 30 - t)
        cnt = jnp.sum((bits >= cand).astype(jnp.int32), keepdims=True)
        return jnp.where(cnt >= cap, cand, thr)

    thr = lax.fori_loop(0, 31, bit_step, jnp.zeros((1, 1), jnp.int32))
    gt = bits > thr
    eq = bits == thr
    need = (cap - jnp.sum(gt.astype(jnp.int32), keepdims=True)).astype(F32)
    eqf = eq.astype(F32)
    w_eq, _, b_eq = _prefix_counts(eqf)
    sel = gt | (eq & ((w_eq - eqf + b_eq) < need))
    within, rowtot, before = _prefix_counts(sel.astype(F32))
    row_incl = before[:, 0:1] + rowtot[:, 0:1]
    slot = lax.broadcasted_iota(jnp.int32, (1, cap), 1).astype(F32)
    done = row_incl <= slot
    r_j = jnp.sum(done.astype(F32), axis=0, keepdims=True)
    off_j = jnp.sum(jnp.where(done, rowtot[:, 0:1], 0.0), axis=0, keepdims=True)
    onehot = lax.broadcasted_iota(jnp.int32, (r, cap), 0).astype(F32) == r_j
    cnt_t = jnp.dot(within.T.astype(BF16), onehot.astype(BF16), preferred_element_type=F32)
    lane_j = jnp.sum((cnt_t <= (slot - off_j)).astype(F32), axis=0, keepdims=True)
    idx_ref[...] = (r_j * LANES + lane_j).astype(jnp.int32)
    aff_t = jnp.dot(aff.T, onehot.astype(F32), preferred_element_type=F32, precision=lax.Precision.HIGHEST)
    lsub = lax.broadcasted_iota(jnp.int32, (LANES, cap), 0).astype(F32)
    gate_ref[...] = jnp.sum(jnp.where(lsub == lane_j, aff_t, 0.0), axis=0, keepdims=True)


def _select(aff3, cap):
    e, r, _ = aff3.shape
    return pl.pallas_call(
        functools.partial(_select_kernel, cap=cap),
        out_shape=[jax.ShapeDtypeStruct((e, 1, cap), jnp.int32), jax.ShapeDtypeStruct((e, 1, cap), F32)],
        grid=(e,),
        in_specs=[pl.BlockSpec((None, r, LANES), lambda i: (i, 0, 0))],
        out_specs=[pl.BlockSpec((None, 1, cap), lambda i: (i, 0, 0)), pl.BlockSpec((None, 1, cap), lambda i: (i, 0, 0))],
        compiler_params=_cparams(1),
        name="select",
    )(aff3)


def _moe_kernel(idx_ref, gate_ref, xt_hbm, y_in_hbm, wg_ref, wu_ref, wd_ref, y_hbm, xbuf, ybuf, sem, *, tile, ff_chunk):
    del y_in_hbm
    step = pl.program_id(0) * pl.num_programs(1) + pl.program_id(1)
    base = step * tile

    def gather(j, carry):
        t = idx_ref[base + j]
        pltpu.make_async_copy(xt_hbm.at[pl.ds(t, 1)], xbuf.at[pl.ds(j, 1)], sem.at[0]).start()
        pltpu.make_async_copy(y_hbm.at[pl.ds(t, 1)], ybuf.at[pl.ds(j, 1)], sem.at[1]).start()
        return carry

    lax.fori_loop(0, tile, gather, 0)
    pltpu.make_async_copy(xt_hbm.at[pl.ds(0, tile)], xbuf, sem.at[0]).wait()
    pltpu.make_async_copy(y_hbm.at[pl.ds(0, tile)], ybuf, sem.at[1]).wait()

    x = xbuf[...].astype(BF16)
    ye = None
    for c0 in range(0, D_FF, ff_chunk):
        hg = jnp.dot(x, wg_ref[:, c0:c0 + ff_chunk], preferred_element_type=F32)
        hu = jnp.dot(x, wu_ref[:, c0:c0 + ff_chunk], preferred_element_type=F32)
        hid = (hg * jax.nn.sigmoid(hg) * hu).astype(BF16)
        part = jnp.dot(hid, wd_ref[c0:c0 + ff_chunk, :], preferred_element_type=F32)
        ye = part if ye is None else ye + part
    ybuf[...] = ybuf[...] + ye * gate_ref[...]

    def scatter(j, carry):
        t = idx_ref[base + j]
        pltpu.make_async_copy(ybuf.at[pl.ds(j, 1)], y_hbm.at[pl.ds(t, 1)], sem.at[2]).start()
        return carry

    lax.fori_loop(0, tile, scatter, 0)
    pltpu.make_async_copy(ybuf, y_hbm.at[pl.ds(0, tile)], sem.at[2]).wait()


def _moe(idx_flat, gate_col, xt, x1, wg, wu, wd, cap):
    n, d = xt.shape
    tile = _pick(cap, 512)
    grid_spec = pltpu.PrefetchScalarGridSpec(
        num_scalar_prefetch=1,
        grid=(N_EXPERTS, cap // tile),
        in_specs=[
            pl.BlockSpec((None, tile, 1), lambda e, s, idx: (e, s, 0)),
            pl.BlockSpec(memory_space=pl.ANY),
            pl.BlockSpec(memory_space=pl.ANY),
            pl.BlockSpec((None, d, D_FF), lambda e, s, idx: (e, 0, 0)),
            pl.BlockSpec((None, d, D_FF), lambda e, s, idx: (e, 0, 0)),
            pl.BlockSpec((None, D_FF, d), lambda e, s, idx: (e, 0, 0)),
        ],
        out_specs=pl.BlockSpec(memory_space=pl.ANY),
        scratch_shapes=[pltpu.VMEM((tile, d), F32), pltpu.VMEM((tile, d), F32), pltpu.SemaphoreType.DMA((3,))],
    )
    return pl.pallas_call(
        functools.partial(_moe_kernel, tile=tile, ff_chunk=512),
        out_shape=jax.ShapeDtypeStruct((n, d), F32),
        grid_spec=grid_spec,
        input_output_aliases={3: 0},
        compiler_params=_cparams(2),
        name="moe",
    )(idx_flat, gate_col, xt, x1, wg, wu, wd)


def _pe_kernel(x_ref, pe_ref, wpg_ref, wpp_ref, gfin_ref, o_ref, *, last):
    x = x_ref[...]
    gate = jax.nn.sigmoid(jnp.dot(x.astype(BF16), wpg_ref[...], preferred_element_type=F32))
    x = x + gate * jnp.dot(pe_ref[...].astype(BF16), wpp_ref[...], preferred_element_type=F32)
    if last:
        x = _rms(x) * gfin_ref[...]
    o_ref[...] = x


def _pe(x2d, pe2d, wpg, wpp, gfin, last):
    n, d = x2d.shape
    tm = _pick(n, 512)
    return pl.pallas_call(
        functools.partial(_pe_kernel, last=last),
        out_shape=jax.ShapeDtypeStruct((n, d), F32),
        grid=(n // tm,),
        in_specs=[pl.BlockSpec((tm, d), lambda i: (i, 0)), pl.BlockSpec((tm, P_DIM), lambda i: (i, 0)),
                  _const_spec(wpg.shape), _const_spec(wpp.shape), _const_spec(gfin.shape)],
        out_specs=pl.BlockSpec((tm, d), lambda i: (i, 0)),
        compiler_params=_cparams(1),
        name="pe",
    )(x2d, pe2d, wpg, wpp, gfin)


def _pack_weights(l, w_in, w_alpha_up, b_alpha_up, norm_c, w_branch, w_router, b_router):
    widths = (512, 512, 512, 512, 128, 128, 256, 256, 512, 2 * GATE_RANK, 512)
    offs = [0]
    for w in widths:
        offs.append(offs[-1] + w)
    wqa, wka, wva, wqb, wkb, wvb, wqc, wkc, wvc, wac, wrc = (w_in[l][:, offs[i]:offs[i + 1]] for i in range(11))
    zeros64 = jnp.zeros((D_MODEL, DH_B), F32)
    qb_slots = []
    for h in range(H_B):
        wh = wqb[:, h * DH_B:(h + 1) * DH_B]
        qb_slots += [wh, zeros64] if h < GQ_B else [zeros64, wh]
    dup = lambda w: jnp.concatenate([jnp.concatenate([w[:, h * DK_C:(h + 1) * DK_C]] * 2, axis=1) for h in range(H_C)], axis=1)
    w_cat = jnp.concatenate(
        [wqa, wka, wva, jnp.concatenate(qb_slots, axis=1), wkb, wvb, dup(wqc), dup(wkc), wvc,
         jnp.pad(wac, ((0, 0), (0, LANES - 2 * GATE_RANK))), wrc], axis=1).astype(BF16)
    wbd = jnp.zeros((H_C, LANES, LANES), F32)
    bup = []
    for h in range(H_C):
        sl = slice(h * DK_C, (h + 1) * DK_C)
        wbd = wbd.at[h, 0:GATE_RANK, 0:DK_C].set(w_alpha_up[l, 0][:, sl])
        wbd = wbd.at[h, GATE_RANK:2 * GATE_RANK, DK_C:].set(w_alpha_up[l, 1][:, sl])
        bup.append(jnp.concatenate([b_alpha_up[l, 0, sl], b_alpha_up[l, 1, sl]])[None, :])
    bup = jnp.stack(bup)
    ng = norm_c[l][:, None, :]
    order = [h for g in range(GQ_B) for h in (g, GQ_B + g)]
    wb1 = jnp.concatenate([w_branch[l, 1][h * DH_B:(h + 1) * DH_B] for h in order], axis=0)
    wbr = jnp.stack([w_branch[l, 0], wb1, w_branch[l, 2]]).astype(BF16)
    wr = jnp.pad(w_router[l], ((0, 0), (0, LANES - N_EXPERTS)))
    br = jnp.pad(b_router[l], (0, LANES - N_EXPERTS))[None, :]
    return w_cat, wbd.astype(BF16), bup, ng, wbr, wr, br


def _trunk(x, pe, rel_bias, g_mix, lam_a, subln_a, sink_b, w_merge_gate, w_out, g_ffn, w_exp_gate, w_exp_up,
           w_exp_down, w_pe_proj, w_pe_gate, g_final, packed):
    bsz, s_len, d = x.shape
    n = bsz * s_len
    cap = EC_FACTOR * n // N_EXPERTS
    bias_a, bias_b = rel_bias[:, :H_A], rel_bias[:, H_A:]
    x2d = x.reshape(n, d)
    for l in range(DEPTH):
        w_cat, wbd, bup, ng, wbr, wr, br = packed[l]
        lam_init = 0.8 - 0.6 * math.exp(-0.3 * l)
        outs = _inproj(x2d, g_mix[l][None, :], w_cat)
        qa, ka, va, qb, kvb, qc, kc, vc, ac, rc = (o.reshape(bsz, s_len, o.shape[-1]) for o in outs)
        oa = _diffattn(qa, ka, va, lam_a[l], subln_a[l][None, :], bias_a, lam_init)
        ob = _winattn(qb, kvb, sink_b[l], bias_b)
        oc = _gla(qc, kc, vc, ac, rc, wbd, bup, ng)
        x1, xt, aff = _merge(x2d, oa.reshape(n, -1), ob.reshape(n, -1), oc.reshape(n, -1), g_mix[l][None, :],
                             w_merge_gate[l], wbr, w_out[l], g_ffn[l][None, :], wr, br)
        aff3 = aff[:, :N_EXPERTS].T.reshape(N_EXPERTS, n // LANES, LANES)
        idx, gates = _select(aff3, cap)
        x2 = _moe(idx.reshape(-1), gates.reshape(N_EXPERTS, cap, 1), xt, x1,
                  w_exp_gate[l], w_exp_up[l], w_exp_down[l], cap)
        x2d = _pe(x2, pe[l].reshape(n, -1), w_pe_gate[l], w_pe_proj[l], g_final[None, :], l == DEPTH - 1)
    return x2d.reshape(bsz, s_len, d)


def kernel(x_prompt, x_sample, p_prompt, p_sample, rel_bias, g_mix, w_in, lam_a, subln_a, sink_b, w_alpha_up,
           b_alpha_up, norm_c, w_branch, w_merge_gate, w_out, g_ffn, w_router, b_router, w_exp_gate, w_exp_up,
           w_exp_down, w_pe_proj, w_pe_gate, g_final):
    packed = [_pack_weights(l, w_in, w_alpha_up, b_alpha_up, norm_c, w_branch, w_router, b_router)
              for l in range(DEPTH)]
    shared = (rel_bias, g_mix, lam_a, subln_a, sink_b, w_merge_gate.astype(BF16), w_out.astype(BF16), g_ffn,
              w_exp_gate.astype(BF16), w_exp_up.astype(BF16), w_exp_down.astype(BF16), w_pe_proj.astype(BF16),
              w_pe_gate.astype(BF16), g_final, packed)
    y_prompt = _trunk(x_prompt, p_prompt, *shared)
    y_sample = _trunk(x_sample, p_sample, *shared)
    return (y_prompt, y_sample)
```

```python
import functools
import math

import jax
import jax.numpy as jnp
from jax import lax
from jax.experimental import pallas as pl
from jax.experimental.pallas import tpu as pltpu

F32 = jnp.float32
BF16 = jnp.bfloat16

D_MODEL = 1024
DEPTH = 2
H_A, DK_A, DV_A = 4, 64, 128
H_B, KV_B, GQ_B, DH_B = 8, 2, 4, 64
WINDOW, BLOCK_B = 128, 128
H_C, DK_C, DV_C = 4, 64, 128
GATE_RANK, GATE_TAU, CHUNK = 16, 16, 64
BRANCH_W, N_BRANCH = 512, 3
N_BUCKETS, MAX_DIST = 32, 128
N_EXPERTS, EC_FACTOR, D_FF = 16, 2, 2048
P_DIM = 256
NORM_EPS = 1e-6
LOG2E = math.log2(math.e)
LANES = 128
VMEM_LIMIT = 56 * 1024 * 1024

_IN_COLS = (
    ("qa", 512, BF16, DK_A ** -0.5 * LOG2E), ("ka", 512, BF16, 1.0), ("va", 512, BF16, 1.0),
    ("qb", 1024, BF16, DH_B ** -0.5), ("kvb", 256, BF16, 1.0),
    ("qc", 512, F32, DK_C ** -0.5), ("kc", 512, F32, 1.0), ("vc", 512, BF16, 1.0),
    ("ac", 128, F32, 1.0), ("rc", 512, BF16, 1.0),
)


def _cparams(n_axes):
    return pltpu.CompilerParams(dimension_semantics=("arbitrary",) * n_axes,
                                vmem_limit_bytes=VMEM_LIMIT)


def _const_spec(shape):
    nd = len(shape)
    return pl.BlockSpec(shape, lambda *_: (0,) * nd, pipeline_mode=pl.Buffered(1))


def _pick(n, pref):
    t = min(n, pref)
    while n % t:
        t //= 2
    return t


def _rms(x):
    return x * lax.rsqrt(jnp.mean(x * x, axis=-1, keepdims=True) + NORM_EPS)


def _t5_bucket(rel):
    half = N_BUCKETS // 2
    max_exact = half // 2
    ret = jnp.where(rel > 0, half, 0)
    n = jnp.abs(rel)
    nf = jnp.maximum(n, 1).astype(F32)
    large = max_exact + (jnp.log(nf / max_exact) / math.log(MAX_DIST / max_exact)
                         * (half - max_exact)).astype(jnp.int32)
    large = jnp.minimum(large, half - 1)
    return ret + jnp.where(n < max_exact, n, large)


def _bias_lookup(table, rel):
    onehot = (_t5_bucket(rel)[..., None] == jnp.arange(N_BUCKETS)).astype(F32)
    return jnp.einsum("...b,bh->...h", onehot, table, precision=lax.Precision.HIGHEST)


def _inproj_kernel(x_ref, g_ref, w_ref, wvt_ref, *out_refs):
    h = (_rms(x_ref[...]) * g_ref[...]).astype(BF16)
    c0 = 0
    for o_ref, (_, width, _, scale) in zip(out_refs, _IN_COLS):
        acc = jnp.dot(h, w_ref[:, c0:c0 + width], preferred_element_type=F32)
        if scale != 1.0:
            acc = acc * scale
        o_ref[...] = acc.astype(o_ref.dtype)
        c0 += width
    vt = lax.dot_general(wvt_ref[...], h, (((1,), (1,)), ((), ())), preferred_element_type=F32)
    out_refs[-1][...] = vt.astype(out_refs[-1].dtype)


def _inproj(x2d, g, w_cat, w_vct):
    n, d = x2d.shape
    tm = _pick(n, 512)
    out_shape = [jax.ShapeDtypeStruct((n, w), dt) for _, w, dt, _ in _IN_COLS]
    out_specs = [pl.BlockSpec((tm, w), lambda i: (i, 0)) for _, w, _, _ in _IN_COLS]
    out_shape.append(jax.ShapeDtypeStruct((w_vct.shape[0], n), BF16))
    out_specs.append(pl.BlockSpec((w_vct.shape[0], tm), lambda i: (0, i)))
    return pl.pallas_call(
        _inproj_kernel,
        out_shape=out_shape,
        grid=(n // tm,),
        in_specs=[pl.BlockSpec((tm, d), lambda i: (i, 0)), _const_spec(g.shape), _const_spec(w_cat.shape),
                  _const_spec(w_vct.shape)],
        out_specs=out_specs,
        compiler_params=_cparams(1),
        name="inproj",
    )(x2d, g, w_cat, w_vct)


def _diffattn_kernel(lam_ref, subg_ref, cst_ref, band_ref, q_ref, k_ref, v_ref, o_ref, s_ref, *, lam_init):
    i = pl.program_id(2)
    nq = pl.num_programs(2)
    tq = q_ref.shape[0]
    s_len = k_ref.shape[0]
    lp = lam_ref[...]
    lam = (jnp.exp(jnp.sum(lp[0:1] * lp[1:2], axis=-1, keepdims=True))
           - jnp.exp(jnp.sum(lp[2:3] * lp[3:4], axis=-1, keepdims=True))) + lam_init
    q = q_ref[...]
    k = k_ref[...]
    lane = lax.broadcasted_iota(jnp.int32, (1, LANES), 1)
    zero = jnp.zeros_like(q)
    col = lax.broadcasted_iota(jnp.int32, (tq, s_len), 1)
    base = jnp.where(col < i * tq, cst_ref[0:1, 0:1], cst_ref[1:2, 0:1])
    for m in range(2):
        qm = jnp.where((lane < DK_A) == (m == 0), q, zero)
        s_ref[m] = lax.dot_general(qm, k, (((1,), (1,)), ((), ())), preferred_element_type=F32) + base
    for d in (-1, 0, 1):
        kb = i + d

        @pl.when((kb >= 0) & (kb < nq))
        def _(d=d, kb=kb):
            off = pl.multiple_of(kb * tq, tq)
            delta = band_ref[d + 1]
            for m in range(2):
                s_ref[m, :, pl.ds(off, tq)] += delta

    probs, inv = [], []
    for m in range(2):
        s = s_ref[m]
        p = jnp.exp2(s - jnp.max(s, axis=-1, keepdims=True))
        probs.append(p)
        inv.append(1.0 / jnp.sum(p, axis=-1, keepdims=True))
    w = (probs[0] * inv[0] - probs[1] * (lam * inv[1])).astype(BF16)
    o = jnp.dot(w, v_ref[...], preferred_element_type=F32)
    o_ref[...] = (_rms(o) * subg_ref[...] * (1.0 - lam_init)).astype(o_ref.dtype)


def _diffattn(qa, ka, va, lam_l, subg_l, bias_a, lam_init):
    bsz, s_len, _ = qa.shape
    tq = _pick(s_len, 256)
    nq = s_len // tq
    ii = jnp.arange(tq)
    rel = jnp.stack([d * tq + ii[None, :] - ii[:, None] for d in (-1, 0, 1)])
    bias_a = bias_a * LOG2E
    table = _bias_lookup(bias_a, rel)
    c_left = bias_a[N_BUCKETS // 2 - 1]
    c_right = bias_a[N_BUCKETS - 1]
    side = jnp.stack([c_left, c_right, c_right])
    band = (table - side[:, None, None, :]).transpose(3, 0, 1, 2)
    cst = jnp.broadcast_to(jnp.stack([c_left, c_right], axis=1)[:, :, None], (H_A, 2, LANES))
    cst = jnp.pad(cst, ((0, 0), (0, 6), (0, 0)))
    return pl.pallas_call(
        functools.partial(_diffattn_kernel, lam_init=lam_init),
        out_shape=jax.ShapeDtypeStruct((bsz, s_len, H_A * DV_A), BF16),
        grid=(bsz, H_A, nq),
        in_specs=[
            _const_spec(lam_l.shape),
            _const_spec(subg_l.shape),
            pl.BlockSpec((None, 8, LANES), lambda b, h, i: (h, 0, 0)),
            pl.BlockSpec((None, 3, tq, tq), lambda b, h, i: (h, 0, 0, 0)),
            pl.BlockSpec((None, tq, LANES), lambda b, h, i: (b, i, h)),
            pl.BlockSpec((None, s_len, LANES), lambda b, h, i: (b, 0, h)),
            pl.BlockSpec((None, s_len, LANES), lambda b, h, i: (b, 0, h)),
        ],
        out_specs=pl.BlockSpec((None, tq, LANES), lambda b, h, i: (b, i, h)),
        scratch_shapes=[pltpu.VMEM((2, tq, s_len), F32)],
        compiler_params=_cparams(3),
        name="diffattn",
    )(lam_l, subg_l, cst, band, qa, ka, va)


def _winattn_kernel(sink_ref, band_ref, q_ref, kp_ref, kc_ref, kn_ref, o_ref, *, s_len):
    i = pl.program_id(1)
    tq = q_ref.shape[0]
    kv = jnp.concatenate([kp_ref[...], kc_ref[...], kn_ref[...]], axis=0)
    kk = kv[:, :LANES]
    vv = kv[:, LANES:]
    kabs = i * tq - BLOCK_B + lax.broadcasted_iota(jnp.int32, (1, tq + 2 * BLOCK_B), 1)
    in_seq = (kabs >= 0) & (kabs < s_len)
    lane = lax.broadcasted_iota(jnp.int32, (1, LANES), 1)
    for g in range(GQ_B):
        outs = []
        for c in range(KV_B):
            h = c * GQ_B + g
            s = lax.dot_general(q_ref[:, h * LANES:(h + 1) * LANES], kk, (((1,), (1,)), ((), ())),
                                preferred_element_type=F32)
            s = jnp.where(in_seq, s + band_ref[h], -jnp.inf)
            sk = sink_ref[h:h + 1, 0:1]
            m = jnp.maximum(jnp.max(s, axis=-1, keepdims=True), sk)
            p = jnp.exp(s - m)
            p = p / (jnp.sum(p, axis=-1, keepdims=True) + jnp.exp(sk - m))
            outs.append(jnp.dot(p.astype(BF16), vv, preferred_element_type=F32))
        o_ref[:, g * LANES:(g + 1) * LANES] = jnp.where(lane < DH_B, outs[0], outs[1]).astype(o_ref.dtype)


def _winattn(qb, kvb, sink_l, bias_b):
    bsz, s_len, _ = qb.shape
    nb = s_len // BLOCK_B
    tq = _pick(s_len, 256)
    r = tq // BLOCK_B
    rel = jnp.arange(tq + 2 * BLOCK_B)[None, :] - BLOCK_B - jnp.arange(tq)[:, None]
    band = jnp.where((jnp.abs(rel) <= WINDOW)[:, :, None], _bias_lookup(bias_b, rel), -jnp.inf)
    band = band.transpose(2, 0, 1)
    sink = jnp.broadcast_to(sink_l[:, None], (H_B, LANES))
    kvw = 2 * KV_B * DH_B
    return pl.pallas_call(
        functools.partial(_winattn_kernel, s_len=s_len),
        out_shape=jax.ShapeDtypeStruct((bsz, s_len, H_B * DH_B), BF16),
        grid=(bsz, s_len // tq),
        in_specs=[
            _const_spec(sink.shape),
            _const_spec(band.shape),
            pl.BlockSpec((None, tq, H_B * LANES), lambda b, i: (b, i, 0)),
            pl.BlockSpec((None, BLOCK_B, kvw), lambda b, i: (b, jnp.maximum(i * r - 1, 0), 0)),
            pl.BlockSpec((None, tq, kvw), lambda b, i: (b, i, 0)),
            pl.BlockSpec((None, BLOCK_B, kvw), lambda b, i: (b, jnp.minimum((i + 1) * r, nb - 1), 0)),
        ],
        out_specs=pl.BlockSpec((None, tq, H_B * DH_B), lambda b, i: (b, i, 0)),
        compiler_params=_cparams(2),
        name="winattn",
    )(sink, band, qb, kvb, kvb, kvb)


def _gla_kernel(q_ref, k_ref, v_ref, vt_ref, a_ref, r_ref, wbd_ref, bup_ref, ng_ref, o_ref,
                g_scr, acc_scr, qd_scr, kv_scr, dec_scr):
    s_len = q_ref.shape[0]
    nc = s_len // CHUNK
    pair = 2 * CHUNK
    pre = jnp.dot(a_ref[...].astype(BF16), wbd_ref[...], preferred_element_type=F32) + bup_ref[...]
    g_scr[...] = jax.nn.log_sigmoid(pre) * (1.0 / GATE_TAU)
    lane = lax.broadcasted_iota(jnp.int32, (1, LANES), 1)
    fwd_lane = lane < DK_C
    first = lax.broadcasted_iota(jnp.int32, (pair, 1), 0) < CHUNK
    ri = lax.broadcasted_iota(jnp.int32, (2 * pair, pair), 0)
    ci = lax.broadcasted_iota(jnp.int32, (2 * pair, pair), 1)
    sh = CHUNK.bit_length() - 1
    blk, rr, cc = ri >> sh, ri & (CHUNK - 1), ci & (CHUNK - 1)
    same = (ci >> sh) == (blk & 1)
    cum = jnp.where(same & (((blk < 2) & (cc <= rr)) | ((blk >= 2) & (cc >= rr))), 1.0, 0.0).astype(BF16)
    qrow = lax.broadcasted_iota(jnp.int32, (pair, LANES), 0) & (CHUNK - 1)
    kcol = lax.broadcasted_iota(jnp.int32, (pair, LANES), 1)
    keep = ((kcol < CHUNK) & (kcol <= qrow)) | ((kcol >= CHUNK) & (kcol - CHUNK >= qrow))
    nt = (((1,), (1,)), ((), ()))

    def phase1(p, carry):
        off = pl.multiple_of(p * pair, pair)
        g = g_scr[pl.ds(off, pair), :]
        g1 = g.astype(BF16)
        r1 = g - g1.astype(F32)
        g2 = r1.astype(BF16)
        g3 = (r1 - g2.astype(F32)).astype(BF16)
        b3 = jnp.dot(cum, jnp.concatenate([g1, g2, g3], axis=1), preferred_element_type=F32)
        bb = b3[:, :LANES] + b3[:, LANES:2 * LANES] + b3[:, 2 * LANES:]
        b = jnp.where(fwd_lane, bb[:pair], bb[pair:])
        t0 = jnp.where(fwd_lane, bb[CHUNK - 1:CHUNK], bb[pair:pair + 1])
        t1 = jnp.where(fwd_lane, bb[pair - 1:pair], bb[pair + CHUNK:pair + CHUNK + 1])
        q = q_ref[pl.ds(off, pair), :]
        k = k_ref[pl.ds(off, pair), :]
        v = v_ref[pl.ds(off, pair), :]
        qd = q * jnp.exp(b)
        kd = k * jnp.exp(-b)
        ki = (k * jnp.exp(jnp.where(first, t0, t1) - b)).astype(BF16)
        qd_f = jnp.where(fwd_lane, qd, 0.0).astype(BF16)
        qd_b = jnp.where(fwd_lane, 0.0, qd).astype(BF16)
        qd_scr[2 * p] = jnp.concatenate([qd_f[:CHUNK], qd_b[:CHUNK]], axis=0)
        qd_scr[2 * p + 1] = jnp.concatenate([qd_f[CHUNK:], qd_b[CHUNK:]], axis=0)
        kd_f = jnp.where(fwd_lane, kd, 0.0).astype(BF16)
        kd_b = jnp.where(fwd_lane, 0.0, kd).astype(BF16)
        kd4 = jnp.concatenate([kd_f[:CHUNK], kd_b[:CHUNK], kd_f[CHUNK:], kd_b[CHUNK:]], axis=0)
        sc = lax.dot_general(qd.astype(BF16), kd4, nt, preferred_element_type=F32)
        a = jnp.where(keep, jnp.where(first, sc[:, :LANES], sc[:, LANES:]), 0.0).astype(BF16)
        zero = jnp.zeros_like(a)
        a4 = jnp.concatenate([jnp.where(first, a, zero), jnp.where(first, zero, a)], axis=1)
        v4 = jnp.concatenate([v[:CHUNK], v[:CHUNK], v[CHUNK:], v[CHUNK:]], axis=0)
        acc_scr[pl.ds(off, pair), :] = jnp.dot(a4, v4, preferred_element_type=F32)
        vt = vt_ref[:, pl.ds(off, pair)]
        vzero = jnp.zeros_like(vt)
        kv_scr[2 * p] = jnp.dot(jnp.where(fwd_lane, vt, vzero), ki, preferred_element_type=F32)
        kv_scr[2 * p + 1] = jnp.dot(jnp.where(fwd_lane, vzero, vt), ki, preferred_element_type=F32)
        dec_scr[2 * p] = jnp.exp(t0)
        dec_scr[2 * p + 1] = jnp.exp(t1)
        return carry

    lax.fori_loop(0, nc // 2, phase1, 0, unroll=2)

    def phase2(t, state):
        cf = t
        cb = nc - 1 - t
        lhs = jnp.concatenate([qd_scr[cf, 0:CHUNK, :], qd_scr[cb, CHUNK:2 * CHUNK, :]], axis=0)
        o2 = lax.dot_general(lhs, state.astype(BF16), nt, preferred_element_type=F32)
        acc_scr[pl.ds(pl.multiple_of(cf * CHUNK, CHUNK), CHUNK), :] += o2[:CHUNK]
        acc_scr[pl.ds(pl.multiple_of(cb * CHUNK, CHUNK), CHUNK), :] += o2[CHUNK:]
        dec = jnp.where(fwd_lane, dec_scr[cf], dec_scr[cb])
        inc = jnp.where(fwd_lane, kv_scr[cf], kv_scr[cb])
        return dec * state + inc

    lax.fori_loop(0, nc, phase2, jnp.zeros((DV_C, LANES), F32), unroll=2)
    r = r_ref[...].astype(F32)
    o_ref[...] = (_rms(acc_scr[...]) * ng_ref[...] * (r * jax.nn.sigmoid(r))).astype(o_ref.dtype)


def _gla(qc, kc, vc, vct, ac, rc, wbd, bup, ng):
    bsz, s_len, _ = qc.shape
    nc = s_len // CHUNK
    blk = lambda: pl.BlockSpec((None, s_len, LANES), lambda b, h: (b, 0, h))
    per_head = lambda r: pl.BlockSpec((None, r, LANES), lambda b, h: (h, 0, 0))
    return pl.pallas_call(
        _gla_kernel,
        out_shape=jax.ShapeDtypeStruct((bsz, s_len, H_C * DV_C), BF16),
        grid=(bsz, H_C),
        in_specs=[blk(), blk(), blk(),
                  pl.BlockSpec((DV_C, s_len), lambda b, h: (h, b)),
                  pl.BlockSpec((None, s_len, LANES), lambda b, h: (b, 0, 0)),
                  blk(), per_head(LANES), per_head(1), per_head(1)],
        out_specs=blk(),
        scratch_shapes=[pltpu.VMEM((s_len, LANES), F32), pltpu.VMEM((s_len, LANES), F32),
                        pltpu.VMEM((nc, 2 * CHUNK, LANES), BF16), pltpu.VMEM((nc, DV_C, LANES), F32),
                        pltpu.VMEM((nc, 1, LANES), F32)],
        compiler_params=_cparams(2),
        name="gla",
    )(qc, kc, vc, vct, ac, rc, wbd, bup, ng)


def _merge_kernel(x_ref, oa_ref, ob_ref, oc_ref, gmix_ref, wmg_ref, wbr_ref, wout_ref, gffn_ref, wr_ref, br_ref,
                  x1_ref, xt_ref, aff_ref):
    x = x_ref[...]
    h = (_rms(x) * gmix_ref[...]).astype(BF16)
    acc = None
    for g, o_ref in enumerate((oa_ref, ob_ref, oc_ref)):
        gate = jax.nn.sigmoid(jnp.dot(h, wmg_ref[:, g * D_MODEL:(g + 1) * D_MODEL], preferred_element_type=F32))
        br = jnp.dot(o_ref[...], wbr_ref[g], preferred_element_type=F32)
        acc = gate * br if acc is None else acc + gate * br
    x1 = x + jnp.dot(acc.astype(BF16), wout_ref[...], preferred_element_type=F32)
    x1_ref[...] = x1
    xt = _rms(x1) * gffn_ref[...]
    xt_ref[...] = xt
    logits = jnp.dot(xt, wr_ref[...], preferred_element_type=F32, precision=lax.Precision.HIGHEST) + br_ref[...]
    lane = lax.broadcasted_iota(jnp.int32, logits.shape, 1)
    logits = jnp.where(lane < N_EXPERTS, logits, -jnp.inf)
    e = jnp.exp(logits - jnp.max(logits, axis=-1, keepdims=True))
    aff_ref[...] = e / jnp.sum(e, axis=-1, keepdims=True)


def _merge(x2d, oa, ob, oc, gmix, wmg, wbr, wout, gffn, wr, br):
    n, d = x2d.shape
    tm = _pick(n, 512)
    row = lambda w: pl.BlockSpec((tm, w), lambda i: (i, 0))
    return pl.pallas_call(
        _merge_kernel,
        out_shape=[jax.ShapeDtypeStruct((n, d), F32), jax.ShapeDtypeStruct((n, d), F32),
                   jax.ShapeDtypeStruct((n, LANES), F32)],
        grid=(n // tm,),
        in_specs=[row(d), row(BRANCH_W), row(BRANCH_W), row(BRANCH_W)]
        + [_const_spec(a.shape) for a in (gmix, wmg, wbr, wout, gffn, wr, br)],
        out_specs=[row(d), row(d), row(LANES)],
        compiler_params=_cparams(1),
        name="merge",
    )(x2d, oa, ob, oc, gmix, wmg, wbr, wout, gffn, wr, br)


def _prefix_counts(m):
    r = m.shape[0]
    li = lax.broadcasted_iota(jnp.int32, (LANES, LANES), 0)
    lj = lax.broadcasted_iota(jnp.int32, (LANES, LANES), 1)
    within = jnp.dot(m.astype(BF16), (li <= lj).astype(BF16), preferred_element_type=F32)
    rowtot = jnp.broadcast_to(within[:, LANES - 1:LANES], (r, LANES))
    ri = lax.broadcasted_iota(jnp.int32, (r, r), 0)
    rj = lax.broadcasted_iota(jnp.int32, (r, r), 1)
    before = jnp.dot((rj < ri).astype(F32), rowtot, preferred_element_type=F32, precision=lax.Precision.HIGHEST)
    return within, rowtot, before


def _select_kernel(aff_ref, idx_ref, gate_ref, *, cap):
    aff = aff_ref[...]
    r = aff.shape[0]
    bits = pltpu.bitcast(aff, jnp.int32)

    def bit_step(t, thr):
        cand = thr | jnp.left_shift(jnp.int32(1), 30 - t)
        cnt = jnp.sum((bits >= cand).astype(jnp.int32), keepdims=True)
        return jnp.where(cnt >= cap, cand, thr)

    thr = lax.fori_loop(0, 31, bit_step, jnp.zeros((1, 1), jnp.int32))
    gt = bits > thr
    eq = bits == thr
    need = (cap - jnp.sum(gt.astype(jnp.int32), keepdims=True)).astype(F32)
    eqf = eq.astype(F32)
    w_eq, _, b_eq = _prefix_counts(eqf)
    sel = gt | (eq & ((w_eq - eqf + b_eq) < need))
    within, rowtot, before = _prefix_counts(sel.astype(F32))
    row_incl = before[:, 0:1] + rowtot[:, 0:1]
    slot = lax.broadcasted_iota(jnp.int32, (1, cap), 1).astype(F32)
    done = row_incl <= slot
    r_j = jnp.sum(done.astype(F32), axis=0, keepdims=True)
    off_j = jnp.sum(jnp.where(done, rowtot[:, 0:1], 0.0), axis=0, keepdims=True)
    onehot = lax.broadcasted_iota(jnp.int32, (r, cap), 0).astype(F32) == r_j
    cnt_t = jnp.dot(within.T.astype(BF16), onehot.astype(BF16), preferred_element_type=F32)
    lane_j = jnp.sum((cnt_t <= (slot - off_j)).astype(F32), axis=0, keepdims=True)
    idx_ref[...] = (r_j * LANES + lane_j).astype(jnp.int32)
    aff_t = jnp.dot(aff.T, onehot.astype(F32), preferred_element_type=F32, precision=lax.Precision.HIGHEST)
    lsub = lax.broadcasted_iota(jnp.int32, (LANES, cap), 0).astype(F32)
    gate_ref[...] = jnp.sum(jnp.where(lsub == lane_j, aff_t, 0.0), axis=0, keepdims=True)


def _select(aff3, cap):
    e, r, _ = aff3.shape
    return pl.pallas_call(
        functools.partial(_select_kernel, cap=cap),
        out_shape=[jax.ShapeDtypeStruct((e, 1, cap), jnp.int32), jax.ShapeDtypeStruct((e, 1, cap), F32)],
        grid=(e,),
        in_specs=[pl.BlockSpec((None, r, LANES), lambda i: (i, 0, 0))],
        out_specs=[pl.BlockSpec((None, 1, cap), lambda i: (i, 0, 0)), pl.BlockSpec((None, 1, cap), lambda i: (i, 0, 0))],
        compiler_params=_cparams(1),
        name="select",
    )(aff3)


def _moe_kernel(idx_ref, gate_ref, xt_hbm, y_in_hbm, wg_ref, wu_ref, wd_ref, y_hbm, xbuf, ybuf, sem, *, tile, ff_chunk):
    del y_in_hbm
    step = pl.program_id(0) * pl.num_programs(1) + pl.program_id(1)
    base = step * tile

    def gather(j, carry):
        t = idx_ref[base + j]
        pltpu.make_async_copy(xt_hbm.at[pl.ds(t, 1)], xbuf.at[pl.ds(j, 1)], sem.at[0]).start()
        pltpu.make_async_copy(y_hbm.at[pl.ds(t, 1)], ybuf.at[pl.ds(j, 1)], sem.at[1]).start()
        return carry

    lax.fori_loop(0, tile, gather, 0, unroll=8)
    pltpu.make_async_copy(xt_hbm.at[pl.ds(0, tile)], xbuf, sem.at[0]).wait()
    pltpu.make_async_copy(y_hbm.at[pl.ds(0, tile)], ybuf, sem.at[1]).wait()

    x = xbuf[...].astype(BF16)
    ye = None
    for c0 in range(0, D_FF, ff_chunk):
        hg = jnp.dot(x, wg_ref[:, c0:c0 + ff_chunk], preferred_element_type=F32)
        hu = jnp.dot(x, wu_ref[:, c0:c0 + ff_chunk], preferred_element_type=F32)
        hid = (hg * jax.nn.sigmoid(hg) * hu).astype(BF16)
        part = jnp.dot(hid, wd_ref[c0:c0 + ff_chunk, :], preferred_element_type=F32)
        ye = part if ye is None else ye + part
    ybuf[...] = ybuf[...] + ye * gate_ref[...]

    def scatter(j, carry):
        t = idx_ref[base + j]
        pltpu.make_async_copy(ybuf.at[pl.ds(j, 1)], y_hbm.at[pl.ds(t, 1)], sem.at[2]).start()
        return carry

    lax.fori_loop(0, tile, scatter, 0, unroll=8)
    pltpu.make_async_copy(ybuf, y_hbm.at[pl.ds(0, tile)], sem.at[2]).wait()


def _moe(idx_flat, gate_col, xt, x1, wg, wu, wd, cap):
    n, d = xt.shape
    tile = _pick(cap, 512)
    grid_spec = pltpu.PrefetchScalarGridSpec(
        num_scalar_prefetch=1,
        grid=(N_EXPERTS, cap // tile),
        in_specs=[
            pl.BlockSpec((None, tile, 1), lambda e, s, idx: (e, s, 0)),
            pl.BlockSpec(memory_space=pl.ANY),
            pl.BlockSpec(memory_space=pl.ANY),
            pl.BlockSpec((None, d, D_FF), lambda e, s, idx: (e, 0, 0)),
            pl.BlockSpec((None, d, D_FF), lambda e, s, idx: (e, 0, 0)),
            pl.BlockSpec((None, D_FF, d), lambda e, s, idx: (e, 0, 0)),
        ],
        out_specs=pl.BlockSpec(memory_space=pl.ANY),
        scratch_shapes=[pltpu.VMEM((tile, d), F32), pltpu.VMEM((tile, d), F32), pltpu.SemaphoreType.DMA((3,))],
    )
    return pl.pallas_call(
        functools.partial(_moe_kernel, tile=tile, ff_chunk=512),
        out_shape=jax.ShapeDtypeStruct((n, d), F32),
        grid_spec=grid_spec,
        input_output_aliases={3: 0},
        compiler_params=_cparams(2),
        name="moe",
    )(idx_flat, gate_col, xt, x1, wg, wu, wd)


def _pe_kernel(x_ref, pe_ref, wpg_ref, wpp_ref, gfin_ref, o_ref, *, last):
    x = x_ref[...]
    gate = jax.nn.sigmoid(jnp.dot(x.astype(BF16), wpg_ref[...], preferred_element_type=F32))
    x = x + gate * jnp.dot(pe_ref[...].astype(BF16), wpp_ref[...], preferred_element_type=F32)
    if last:
        x = _rms(x) * gfin_ref[...]
    o_ref[...] = x


def _pe(x2d, pe2d, wpg, wpp, gfin, last):
    n, d = x2d.shape
    tm = _pick(n, 512)
    return pl.pallas_call(
        functools.partial(_pe_kernel, last=last),
        out_shape=jax.ShapeDtypeStruct((n, d), F32),
        grid=(n // tm,),
        in_specs=[pl.BlockSpec((tm, d), lambda i: (i, 0)), pl.BlockSpec((tm, P_DIM), lambda i: (i, 0)),
                  _const_spec(wpg.shape), _const_spec(wpp.shape), _const_spec(gfin.shape)],
        out_specs=pl.BlockSpec((tm, d), lambda i: (i, 0)),
        compiler_params=_cparams(1),
        name="pe",
    )(x2d, pe2d, wpg, wpp, gfin)


def _pack_weights(l, w_in, w_alpha_up, b_alpha_up, norm_c, w_branch, w_router, b_router):
    widths = (512, 512, 512, 512, 128, 128, 256, 256, 512, 2 * GATE_RANK, 512)
    offs = [0]
    for w in widths:
        offs.append(offs[-1] + w)
    wqa, wka, wva, wqb, wkb, wvb, wqc, wkc, wvc, wac, wrc = (w_in[l][:, offs[i]:offs[i + 1]] for i in range(11))
    zeros64 = jnp.zeros((D_MODEL, DH_B), F32)
    qb_slots = []
    for h in range(H_B):
        wh = wqb[:, h * DH_B:(h + 1) * DH_B]
        qb_slots += [wh, zeros64] if h < GQ_B else [zeros64, wh]
    dup = lambda w: jnp.concatenate([jnp.concatenate([w[:, h * DK_C:(h + 1) * DK_C]] * 2, axis=1) for h in range(H_C)], axis=1)
    w_cat = jnp.concatenate(
        [wqa, wka, wva, jnp.concatenate(qb_slots, axis=1), wkb, wvb, dup(wqc), dup(wkc), wvc,
         jnp.pad(wac, ((0, 0), (0, LANES - 2 * GATE_RANK))), wrc], axis=1).astype(BF16)
    wbd = jnp.zeros((H_C, LANES, LANES), F32)
    bup = []
    for h in range(H_C):
        sl = slice(h * DK_C, (h + 1) * DK_C)
        wbd = wbd.at[h, 0:GATE_RANK, 0:DK_C].set(w_alpha_up[l, 0][:, sl])
        wbd = wbd.at[h, GATE_RANK:2 * GATE_RANK, DK_C:].set(w_alpha_up[l, 1][:, sl])
        bup.append(jnp.concatenate([b_alpha_up[l, 0, sl], b_alpha_up[l, 1, sl]])[None, :])
    bup = jnp.stack(bup)
    ng = norm_c[l][:, None, :]
    order = [h for g in range(GQ_B) for h in (g, GQ_B + g)]
    wb1 = jnp.concatenate([w_branch[l, 1][h * DH_B:(h + 1) * DH_B] for h in order], axis=0)
    wbr = jnp.stack([w_branch[l, 0], wb1, w_branch[l, 2]]).astype(BF16)
    wr = jnp.pad(w_router[l], ((0, 0), (0, LANES - N_EXPERTS)))
    br = jnp.pad(b_router[l], (0, LANES - N_EXPERTS))[None, :]
    return w_cat, wvc.T.astype(BF16), wbd.astype(BF16), bup, ng, wbr, wr, br


def _trunk(x, pe, rel_bias, g_mix, lam_a, subln_a, sink_b, w_merge_gate, w_out, g_ffn, w_exp_gate, w_exp_up,
           w_exp_down, w_pe_proj, w_pe_gate, g_final, packed):
    bsz, s_len, d = x.shape
    n = bsz * s_len
    cap = EC_FACTOR * n // N_EXPERTS
    bias_a, bias_b = rel_bias[:, :H_A], rel_bias[:, H_A:]
    x2d = x.reshape(n, d)
    for l in range(DEPTH):
        w_cat, w_vct, wbd, bup, ng, wbr, wr, br = packed[l]
        lam_init = 0.8 - 0.6 * math.exp(-0.3 * l)
        *outs, vct = _inproj(x2d, g_mix[l][None, :], w_cat, w_vct)
        qa, ka, va, qb, kvb, qc, kc, vc, ac, rc = (o.reshape(bsz, s_len, o.shape[-1]) for o in outs)
        oa = _diffattn(qa, ka, va, lam_a[l], subln_a[l][None, :], bias_a, lam_init)
        ob = _winattn(qb, kvb, sink_b[l], bias_b)
        oc = _gla(qc, kc, vc, vct, ac, rc, wbd, bup, ng)
        x1, xt, aff = _merge(x2d, oa.reshape(n, -1), ob.reshape(n, -1), oc.reshape(n, -1), g_mix[l][None, :],
                             w_merge_gate[l], wbr, w_out[l], g_ffn[l][None, :], wr, br)
        aff3 = aff[:, :N_EXPERTS].T.reshape(N_EXPERTS, n // LANES, LANES)
        idx, gates = _select(aff3, cap)
        x2 = _moe(idx.reshape(-1), gates.reshape(N_EXPERTS, cap, 1), xt, x1,
                  w_exp_gate[l], w_exp_up[l], w_exp_down[l], cap)
        x2d = _pe(x2, pe[l].reshape(n, -1), w_pe_gate[l], w_pe_proj[l], g_final[None, :], l == DEPTH - 1)
    return x2d.reshape(bsz, s_len, d)


def kernel(x_prompt, x_sample, p_prompt, p_sample, rel_bias, g_mix, w_in, lam_a, subln_a, sink_b, w_alpha_up,
           b_alpha_up, norm_c, w_branch, w_merge_gate, w_out, g_ffn, w_router, b_router, w_exp_gate, w_exp_up,
           w_exp_down, w_pe_proj, w_pe_gate, g_final):
    packed = [_pack_weights(l, w_in, w_alpha_up, b_alpha_up, norm_c, w_branch, w_router, b_router)
              for l in range(DEPTH)]
    shared = (rel_bias, g_mix, lam_a, subln_a, sink_b, w_merge_gate.astype(BF16), w_out.astype(BF16), g_ffn,
              w_exp_gate.astype(BF16), w_exp_up.astype(BF16), w_exp_down.astype(BF16), w_pe_proj.astype(BF16),
              w_pe_gate.astype(BF16), g_final, packed)
    y_prompt = _trunk(x_prompt, p_prompt, *shared)
    y_sample = _trunk(x_sample, p_sample, *shared)
    return (y_prompt, y_sample)
```

```python
import functools
import math

import jax
import jax.numpy as jnp
from jax import lax
from jax.experimental import pallas as pl
from jax.experimental.pallas import tpu as pltpu

F32 = jnp.float32
BF16 = jnp.bfloat16

D_MODEL = 1024
DEPTH = 2
H_A, DK_A, DV_A = 4, 64, 128
H_B, KV_B, GQ_B, DH_B = 8, 2, 4, 64
WINDOW, BLOCK_B = 128, 128
H_C, DK_C, DV_C = 4, 64, 128
GATE_RANK, GATE_TAU, CHUNK = 16, 16, 64
BRANCH_W, N_BRANCH = 512, 3
N_BUCKETS, MAX_DIST = 32, 128
N_EXPERTS, EC_FACTOR, D_FF = 16, 2, 2048
P_DIM = 256
NORM_EPS = 1e-6
LOG2E = math.log2(math.e)
LANES = 128
VMEM_LIMIT = 56 * 1024 * 1024
MOE_TILE = 512

_IN_COLS = (
    ("qa", 512, BF16, DK_A ** -0.5 * LOG2E), ("ka", 512, BF16, 1.0), ("va", 512, BF16, 1.0),
    ("qb", 1024, BF16, DH_B ** -0.5), ("kvb", 256, BF16, 1.0),
    ("qc", 512, F32, DK_C ** -0.5), ("kc", 512, F32, 1.0), ("vc", 512, BF16, 1.0),
    ("ac", 128, F32, 1.0), ("rc", 512, BF16, 1.0),
)


def _cparams(n_axes):
    return pltpu.CompilerParams(dimension_semantics=("arbitrary",) * n_axes,
                                vmem_limit_bytes=VMEM_LIMIT)


def _const_spec(shape):
    nd = len(shape)
    return pl.BlockSpec(shape, lambda *_: (0,) * nd, pipeline_mode=pl.Buffered(1))


def _pick(n, pref):
    t = min(n, pref)
    while n % t:
        t //= 2
    return t


def _rms(x):
    return x * lax.rsqrt(jnp.mean(x * x, axis=-1, keepdims=True) + NORM_EPS)


def _t5_bucket(rel):
    half = N_BUCKETS // 2
    max_exact = half // 2
    ret = jnp.where(rel > 0, half, 0)
    n = jnp.abs(rel)
    nf = jnp.maximum(n, 1).astype(F32)
    large = max_exact + (jnp.log(nf / max_exact) / math.log(MAX_DIST / max_exact)
                         * (half - max_exact)).astype(jnp.int32)
    large = jnp.minimum(large, half - 1)
    return ret + jnp.where(n < max_exact, n, large)


def _bias_lookup(table, rel):
    onehot = (_t5_bucket(rel)[..., None] == jnp.arange(N_BUCKETS)).astype(F32)
    return jnp.einsum("...b,bh->...h", onehot, table, precision=lax.Precision.HIGHEST)


def _inproj_kernel(x_ref, g_ref, w_ref, wvt_ref, *out_refs):
    h = (_rms(x_ref[...]) * g_ref[...]).astype(BF16)
    c0 = 0
    for o_ref, (_, width, _, scale) in zip(out_refs, _IN_COLS):
        acc = jnp.dot(h, w_ref[:, c0:c0 + width], preferred_element_type=F32)
        if scale != 1.0:
            acc = acc * scale
        o_ref[...] = acc.astype(o_ref.dtype)
        c0 += width
    vt = lax.dot_general(wvt_ref[...], h, (((1,), (1,)), ((), ())), preferred_element_type=F32)
    out_refs[-1][...] = vt.astype(out_refs[-1].dtype)


def _inproj(x2d, g, w_cat, w_vct):
    n, d = x2d.shape
    tm = _pick(n, 512)
    out_shape = [jax.ShapeDtypeStruct((n, w), dt) for _, w, dt, _ in _IN_COLS]
    out_specs = [pl.BlockSpec((tm, w), lambda i: (i, 0)) for _, w, _, _ in _IN_COLS]
    out_shape.append(jax.ShapeDtypeStruct((w_vct.shape[0], n), BF16))
    out_specs.append(pl.BlockSpec((w_vct.shape[0], tm), lambda i: (0, i)))
    return pl.pallas_call(
        _inproj_kernel,
        out_shape=out_shape,
        grid=(n // tm,),
        in_specs=[pl.BlockSpec((tm, d), lambda i: (i, 0)), _const_spec(g.shape), _const_spec(w_cat.shape),
                  _const_spec(w_vct.shape)],
        out_specs=out_specs,
        compiler_params=_cparams(1),
        name="inproj",
    )(x2d, g, w_cat, w_vct)


def _diffattn_kernel(lam_ref, subg_ref, cst_ref, band_ref, q_ref, k_ref, v_ref, o_ref, s_ref, *, lam_init):
    i = pl.program_id(2)
    nq = pl.num_programs(2)
    tq = q_ref.shape[0]
    s_len = k_ref.shape[0]
    lp = lam_ref[...]
    lam = (jnp.exp(jnp.sum(lp[0:1] * lp[1:2], axis=-1, keepdims=True))
           - jnp.exp(jnp.sum(lp[2:3] * lp[3:4], axis=-1, keepdims=True))) + lam_init
    q = q_ref[...]
    k = k_ref[...]
    lane = lax.broadcasted_iota(jnp.int32, (1, LANES), 1)
    zero = jnp.zeros_like(q)
    col = lax.broadcasted_iota(jnp.int32, (tq, s_len), 1)
    base = jnp.where(col < i * tq, cst_ref[0:1, 0:1], cst_ref[1:2, 0:1])
    for m in range(2):
        qm = jnp.where((lane < DK_A) == (m == 0), q, zero)
        s_ref[m] = lax.dot_general(qm, k, (((1,), (1,)), ((), ())), preferred_element_type=F32) + base
    for d in (-1, 0, 1):
        kb = i + d

        @pl.when((kb >= 0) & (kb < nq))
        def _(d=d, kb=kb):
            off = pl.multiple_of(kb * tq, tq)
            delta = band_ref[d + 1]
            for m in range(2):
                s_ref[m, :, pl.ds(off, tq)] += delta

    probs, inv = [], []
    for m in range(2):
        s = s_ref[m]
        p = jnp.exp2(s - jnp.max(s, axis=-1, keepdims=True))
        probs.append(p)
        inv.append(1.0 / jnp.sum(p, axis=-1, keepdims=True))
    w = (probs[0] * inv[0] - probs[1] * (lam * inv[1])).astype(BF16)
    o = jnp.dot(w, v_ref[...], preferred_element_type=F32)
    o_ref[...] = (_rms(o) * subg_ref[...] * (1.0 - lam_init)).astype(o_ref.dtype)


def _diffattn(qa, ka, va, lam_l, subg_l, bias_a, lam_init):
    bsz, s_len, _ = qa.shape
    tq = _pick(s_len, 256)
    nq = s_len // tq
    ii = jnp.arange(tq)
    rel = jnp.stack([d * tq + ii[None, :] - ii[:, None] for d in (-1, 0, 1)])
    bias_a = bias_a * LOG2E
    table = _bias_lookup(bias_a, rel)
    c_left = bias_a[N_BUCKETS // 2 - 1]
    c_right = bias_a[N_BUCKETS - 1]
    side = jnp.stack([c_left, c_right, c_right])
    band = (table - side[:, None, None, :]).transpose(3, 0, 1, 2)
    cst = jnp.broadcast_to(jnp.stack([c_left, c_right], axis=1)[:, :, None], (H_A, 2, LANES))
    cst = jnp.pad(cst, ((0, 0), (0, 6), (0, 0)))
    return pl.pallas_call(
        functools.partial(_diffattn_kernel, lam_init=lam_init),
        out_shape=jax.ShapeDtypeStruct((bsz, s_len, H_A * DV_A), BF16),
        grid=(bsz, H_A, nq),
        in_specs=[
            _const_spec(lam_l.shape),
            _const_spec(subg_l.shape),
            pl.BlockSpec((None, 8, LANES), lambda b, h, i: (h, 0, 0)),
            pl.BlockSpec((None, 3, tq, tq), lambda b, h, i: (h, 0, 0, 0)),
            pl.BlockSpec((None, tq, LANES), lambda b, h, i: (b, i, h)),
            pl.BlockSpec((None, s_len, LANES), lambda b, h, i: (b, 0, h)),
            pl.BlockSpec((None, s_len, LANES), lambda b, h, i: (b, 0, h)),
        ],
        out_specs=pl.BlockSpec((None, tq, LANES), lambda b, h, i: (b, i, h)),
        scratch_shapes=[pltpu.VMEM((2, tq, s_len), F32)],
        compiler_params=_cparams(3),
        name="diffattn",
    )(lam_l, subg_l, cst, band, qa, ka, va)


def _winattn_kernel(sink_ref, band_ref, q_ref, kp_ref, kc_ref, kn_ref, o_ref, *, s_len):
    i = pl.program_id(1)
    tq = q_ref.shape[0]
    kv = jnp.concatenate([kp_ref[...], kc_ref[...], kn_ref[...]], axis=0)
    kk = kv[:, :LANES]
    vv = kv[:, LANES:]
    kabs = i * tq - BLOCK_B + lax.broadcasted_iota(jnp.int32, (1, tq + 2 * BLOCK_B), 1)
    in_seq = (kabs >= 0) & (kabs < s_len)
    lane = lax.broadcasted_iota(jnp.int32, (1, LANES), 1)
    row = lax.broadcasted_iota(jnp.int32, (GQ_B * tq, 1), 0)
    outs = []
    for c in range(KV_B):
        qs = jnp.concatenate([q_ref[:, h * LANES:(h + 1) * LANES] for h in range(c * GQ_B, (c + 1) * GQ_B)], axis=0)
        s = lax.dot_general(qs, kk, (((1,), (1,)), ((), ())), preferred_element_type=F32)
        s = jnp.where(in_seq, s + band_ref[c], -jnp.inf)
        sk = sink_ref[c * GQ_B:c * GQ_B + 1, 0:1]
        for g in range(1, GQ_B):
            sk = jnp.where(row >= g * tq, sink_ref[c * GQ_B + g:c * GQ_B + g + 1, 0:1], sk)
        m = jnp.maximum(jnp.max(s, axis=-1, keepdims=True), sk)
        p = jnp.exp(s - m)
        inv = 1.0 / (jnp.sum(p, axis=-1, keepdims=True) + jnp.exp(sk - m))
        outs.append(jnp.dot(p.astype(BF16), vv, preferred_element_type=F32) * inv)
    for g in range(GQ_B):
        o_ref[:, g * LANES:(g + 1) * LANES] = jnp.where(
            lane < DH_B, outs[0][g * tq:(g + 1) * tq], outs[1][g * tq:(g + 1) * tq]).astype(o_ref.dtype)


def _winattn(qb, kvb, sink_l, bias_b):
    bsz, s_len, _ = qb.shape
    nb = s_len // BLOCK_B
    tq = _pick(s_len, 512)
    r = tq // BLOCK_B
    rel = jnp.arange(tq + 2 * BLOCK_B)[None, :] - BLOCK_B - jnp.arange(tq)[:, None]
    band = jnp.where((jnp.abs(rel) <= WINDOW)[:, :, None], _bias_lookup(bias_b, rel), -jnp.inf)
    band = band.transpose(2, 0, 1).reshape(KV_B, GQ_B * tq, tq + 2 * BLOCK_B)
    sink = jnp.broadcast_to(sink_l[:, None], (H_B, LANES))
    kvw = 2 * KV_B * DH_B
    return pl.pallas_call(
        functools.partial(_winattn_kernel, s_len=s_len),
        out_shape=jax.ShapeDtypeStruct((bsz, s_len, H_B * DH_B), BF16),
        grid=(bsz, s_len // tq),
        in_specs=[
            _const_spec(sink.shape),
            _const_spec(band.shape),
            pl.BlockSpec((None, tq, H_B * LANES), lambda b, i: (b, i, 0)),
            pl.BlockSpec((None, BLOCK_B, kvw), lambda b, i: (b, jnp.maximum(i * r - 1, 0), 0)),
            pl.BlockSpec((None, tq, kvw), lambda b, i: (b, i, 0)),
            pl.BlockSpec((None, BLOCK_B, kvw), lambda b, i: (b, jnp.minimum((i + 1) * r, nb - 1), 0)),
        ],
        out_specs=pl.BlockSpec((None, tq, H_B * DH_B), lambda b, i: (b, i, 0)),
        compiler_params=_cparams(2),
        name="winattn",
    )(sink, band, qb, kvb, kvb, kvb)


def _gla_kernel(q_ref, k_ref, v_ref, vt_ref, a_ref, r_ref, wbd_ref, bup_ref, ng_ref, o_ref,
                g_scr, acc_scr, qd_scr, kv_scr, dec_scr):
    s_len = q_ref.shape[0]
    nc = s_len // CHUNK
    pair = 2 * CHUNK
    pre = jnp.dot(a_ref[...].astype(BF16), wbd_ref[...], preferred_element_type=F32) + bup_ref[...]
    g_scr[...] = jax.nn.log_sigmoid(pre) * (1.0 / GATE_TAU)
    lane = lax.broadcasted_iota(jnp.int32, (1, LANES), 1)
    fwd_lane = lane < DK_C
    first = lax.broadcasted_iota(jnp.int32, (pair, 1), 0) < CHUNK
    ri = lax.broadcasted_iota(jnp.int32, (2 * pair, pair), 0)
    ci = lax.broadcasted_iota(jnp.int32, (2 * pair, pair), 1)
    sh = CHUNK.bit_length() - 1
    blk, rr, cc = ri >> sh, ri & (CHUNK - 1), ci & (CHUNK - 1)
    same = (ci >> sh) == (blk & 1)
    cum = jnp.where(same & (((blk < 2) & (cc <= rr)) | ((blk >= 2) & (cc >= rr))), 1.0, 0.0).astype(BF16)
    qrow = lax.broadcasted_iota(jnp.int32, (pair, LANES), 0) & (CHUNK - 1)
    kcol = lax.broadcasted_iota(jnp.int32, (pair, LANES), 1)
    keep = ((kcol < CHUNK) & (kcol <= qrow)) | ((kcol >= CHUNK) & (kcol - CHUNK >= qrow))
    nt = (((1,), (1,)), ((), ()))

    def phase1(p, carry):
        off = pl.multiple_of(p * pair, pair)
        g = g_scr[pl.ds(off, pair), :]
        g1 = g.astype(BF16)
        r1 = g - g1.astype(F32)
        g2 = r1.astype(BF16)
        g3 = (r1 - g2.astype(F32)).astype(BF16)
        b3 = jnp.dot(cum, jnp.concatenate([g1, g2, g3], axis=1), preferred_element_type=F32)
        bb = b3[:, :LANES] + b3[:, LANES:2 * LANES] + b3[:, 2 * LANES:]
        b = jnp.where(fwd_lane, bb[:pair], bb[pair:])
        t0 = jnp.where(fwd_lane, bb[CHUNK - 1:CHUNK], bb[pair:pair + 1])
        t1 = jnp.where(fwd_lane, bb[pair - 1:pair], bb[pair + CHUNK:pair + CHUNK + 1])
        q = q_ref[pl.ds(off, pair), :]
        k = k_ref[pl.ds(off, pair), :]
        v = v_ref[pl.ds(off, pair), :]
        qd = q * jnp.exp(b)
        kd = k * jnp.exp(-b)
        ki = (k * jnp.exp(jnp.where(first, t0, t1) - b)).astype(BF16)
        qd_f = jnp.where(fwd_lane, qd, 0.0).astype(BF16)
        qd_b = jnp.where(fwd_lane, 0.0, qd).astype(BF16)
        qd_scr[2 * p] = jnp.concatenate([qd_f[:CHUNK], qd_b[:CHUNK]], axis=0)
        qd_scr[2 * p + 1] = jnp.concatenate([qd_f[CHUNK:], qd_b[CHUNK:]], axis=0)
        kd_f = jnp.where(fwd_lane, kd, 0.0).astype(BF16)
        kd_b = jnp.where(fwd_lane, 0.0, kd).astype(BF16)
        kd4 = jnp.concatenate([kd_f[:CHUNK], kd_b[:CHUNK], kd_f[CHUNK:], kd_b[CHUNK:]], axis=0)
        sc = lax.dot_general(qd.astype(BF16), kd4, nt, preferred_element_type=F32)
        a = jnp.where(keep, jnp.where(first, sc[:, :LANES], sc[:, LANES:]), 0.0).astype(BF16)
        zero = jnp.zeros_like(a)
        a4 = jnp.concatenate([jnp.where(first, a, zero), jnp.where(first, zero, a)], axis=1)
        v4 = jnp.concatenate([v[:CHUNK], v[:CHUNK], v[CHUNK:], v[CHUNK:]], axis=0)
        acc_scr[pl.ds(off, pair), :] = jnp.dot(a4, v4, preferred_element_type=F32)
        vt = vt_ref[:, pl.ds(off, pair)]
        vzero = jnp.zeros_like(vt)
        kv_scr[2 * p] = jnp.dot(jnp.where(fwd_lane, vt, vzero), ki, preferred_element_type=F32)
        kv_scr[2 * p + 1] = jnp.dot(jnp.where(fwd_lane, vzero, vt), ki, preferred_element_type=F32)
        dec_scr[2 * p] = jnp.exp(t0)
        dec_scr[2 * p + 1] = jnp.exp(t1)
        return carry

    lax.fori_loop(0, nc // 2, phase1, 0, unroll=2)

    def phase2(t, state):
        cf = t
        cb = nc - 1 - t
        lhs = jnp.concatenate([qd_scr[cf, 0:CHUNK, :], qd_scr[cb, CHUNK:2 * CHUNK, :]], axis=0)
        o2 = lax.dot_general(lhs, state.astype(BF16), nt, preferred_element_type=F32)
        acc_scr[pl.ds(pl.multiple_of(cf * CHUNK, CHUNK), CHUNK), :] += o2[:CHUNK]
        acc_scr[pl.ds(pl.multiple_of(cb * CHUNK, CHUNK), CHUNK), :] += o2[CHUNK:]
        dec = jnp.where(fwd_lane, dec_scr[cf], dec_scr[cb])
        inc = jnp.where(fwd_lane, kv_scr[cf], kv_scr[cb])
        return dec * state + inc

    lax.fori_loop(0, nc, phase2, jnp.zeros((DV_C, LANES), F32), unroll=2)
    r = r_ref[...].astype(F32)
    o_ref[...] = (_rms(acc_scr[...]) * ng_ref[...] * (r * jax.nn.sigmoid(r))).astype(o_ref.dtype)


def _gla(qc, kc, vc, vct, ac, rc, wbd, bup, ng):
    bsz, s_len, _ = qc.shape
    nc = s_len // CHUNK
    blk = lambda: pl.BlockSpec((None, s_len, LANES), lambda b, h: (b, 0, h))
    per_head = lambda r: pl.BlockSpec((None, r, LANES), lambda b, h: (h, 0, 0))
    return pl.pallas_call(
        _gla_kernel,
        out_shape=jax.ShapeDtypeStruct((bsz, s_len, H_C * DV_C), BF16),
        grid=(bsz, H_C),
        in_specs=[blk(), blk(), blk(),
                  pl.BlockSpec((DV_C, s_len), lambda b, h: (h, b)),
                  pl.BlockSpec((None, s_len, LANES), lambda b, h: (b, 0, 0)),
                  blk(), per_head(LANES), per_head(1), per_head(1)],
        out_specs=blk(),
        scratch_shapes=[pltpu.VMEM((s_len, LANES), F32), pltpu.VMEM((s_len, LANES), F32),
                        pltpu.VMEM((nc, 2 * CHUNK, LANES), BF16), pltpu.VMEM((nc, DV_C, LANES), F32),
                        pltpu.VMEM((nc, 1, LANES), F32)],
        compiler_params=_cparams(2),
        name="gla",
    )(qc, kc, vc, vct, ac, rc, wbd, bup, ng)


def _merge_kernel(x_ref, oa_ref, ob_ref, oc_ref, gmix_ref, wmg_ref, wbr_ref, wout_ref, gffn_ref, wr_ref, br_ref,
                  z_ref, aff_ref):
    x = x_ref[...]
    d = x.shape[-1]
    h = (_rms(x) * gmix_ref[...]).astype(BF16)
    acc = None
    for g, o_ref in enumerate((oa_ref, ob_ref, oc_ref)):
        gate = jax.nn.sigmoid(jnp.dot(h, wmg_ref[:, g * D_MODEL:(g + 1) * D_MODEL], preferred_element_type=F32))
        br = jnp.dot(o_ref[...], wbr_ref[g], preferred_element_type=F32)
        acc = gate * br if acc is None else acc + gate * br
    x1 = x + jnp.dot(acc.astype(BF16), wout_ref[...], preferred_element_type=F32)
    xt = _rms(x1) * gffn_ref[...]
    z_ref[:, :d] = x1
    z_ref[:, d:] = xt
    logits = jnp.dot(xt.astype(BF16), wr_ref[...], preferred_element_type=F32) + br_ref[...]
    lane = lax.broadcasted_iota(jnp.int32, logits.shape, 1)
    logits = jnp.where(lane < N_EXPERTS, logits, -jnp.inf)
    e = jnp.exp(logits - jnp.max(logits, axis=-1, keepdims=True))
    aff_ref[...] = e / jnp.sum(e, axis=-1, keepdims=True)


def _merge(x2d, oa, ob, oc, gmix, wmg, wbr, wout, gffn, wr, br):
    n, d = x2d.shape
    tm = _pick(n, 512)
    row = lambda w: pl.BlockSpec((tm, w), lambda i: (i, 0))
    return pl.pallas_call(
        _merge_kernel,
        out_shape=[jax.ShapeDtypeStruct((n, 2 * d), F32), jax.ShapeDtypeStruct((n, LANES), F32)],
        grid=(n // tm,),
        in_specs=[row(d), row(BRANCH_W), row(BRANCH_W), row(BRANCH_W)]
        + [_const_spec(a.shape) for a in (gmix, wmg, wbr, wout, gffn, wr, br)],
        out_specs=[row(2 * d), row(LANES)],
        compiler_params=_cparams(1),
        name="merge",
    )(x2d, oa, ob, oc, gmix, wmg, wbr, wout, gffn, wr, br)


def _prefix_counts(m):
    r = m.shape[0]
    li = lax.broadcasted_iota(jnp.int32, (LANES, LANES), 0)
    lj = lax.broadcasted_iota(jnp.int32, (LANES, LANES), 1)
    within = jnp.dot(m.astype(BF16), (li <= lj).astype(BF16), preferred_element_type=F32)
    rowtot = jnp.broadcast_to(within[:, LANES - 1:LANES], (r, LANES))
    ri = lax.broadcasted_iota(jnp.int32, (r, r), 0)
    rj = lax.broadcasted_iota(jnp.int32, (r, r), 1)
    before = jnp.dot((rj < ri).astype(F32), rowtot, preferred_element_type=F32, precision=lax.Precision.HIGHEST)
    return within, rowtot, before


def _select_kernel(aff_ref, idx_ref, gate_ref, *, cap):
    aff = aff_ref[...]
    r = aff.shape[0]
    bits = pltpu.bitcast(aff, jnp.int32)

    def bit_step(t, thr):
        cand = thr | jnp.left_shift(jnp.int32(1), 30 - t)
        cnt = jnp.sum((bits >= cand).astype(jnp.int32), keepdims=True)
        return jnp.where(cnt >= cap, cand, thr)

    thr = lax.fori_loop(0, 31, bit_step, jnp.zeros((1, 1), jnp.int32))
    gt = bits > thr
    eq = bits == thr
    need = (cap - jnp.sum(gt.astype(jnp.int32), keepdims=True)).astype(F32)
    eqf = eq.astype(F32)
    w_eq, _, b_eq = _prefix_counts(eqf)
    sel = gt | (eq & ((w_eq - eqf + b_eq) < need))
    within, rowtot, before = _prefix_counts(sel.astype(F32))
    row_incl = before[:, 0:1] + rowtot[:, 0:1]
    slot = lax.broadcasted_iota(jnp.int32, (1, cap), 1).astype(F32)
    done = row_incl <= slot
    r_j = jnp.sum(done.astype(F32), axis=0, keepdims=True)
    off_j = jnp.sum(jnp.where(done, rowtot[:, 0:1], 0.0), axis=0, keepdims=True)
    onehot = lax.broadcasted_iota(jnp.int32, (r, cap), 0).astype(F32) == r_j
    cnt_t = jnp.dot(within.T.astype(BF16), onehot.astype(BF16), preferred_element_type=F32)
    lane_j = jnp.sum((cnt_t <= (slot - off_j)).astype(F32), axis=0, keepdims=True)
    idx_ref[...] = (r_j * LANES + lane_j).astype(jnp.int32)
    aff_t = jnp.dot(aff.T, onehot.astype(F32), preferred_element_type=F32, precision=lax.Precision.HIGHEST)
    lsub = lax.broadcasted_iota(jnp.int32, (LANES, cap), 0).astype(F32)
    gate_ref[...] = jnp.sum(jnp.where(lsub == lane_j, aff_t, 0.0), axis=0, keepdims=True)


def _select(aff3, cap):
    e, r, _ = aff3.shape
    return pl.pallas_call(
        functools.partial(_select_kernel, cap=cap),
        out_shape=[jax.ShapeDtypeStruct((e, 1, cap), jnp.int32), jax.ShapeDtypeStruct((e, 1, cap), F32)],
        grid=(e,),
        in_specs=[pl.BlockSpec((None, r, LANES), lambda i: (i, 0, 0))],
        out_specs=[pl.BlockSpec((None, 1, cap), lambda i: (i, 0, 0)), pl.BlockSpec((None, 1, cap), lambda i: (i, 0, 0))],
        compiler_params=_cparams(1),
        name="select",
    )(aff3)


def _moe_kernel(idx_ref, gate_ref, z_in_hbm, wg_ref, wu_ref, wd_ref, z_hbm, zbuf, obuf, gsem, ssem, *, tile, ff_chunk):
    del z_in_hbm
    s = pl.program_id(1)
    nt = pl.num_programs(1)
    d = obuf.shape[-1]
    slot = s % 2

    def issue_gather(step, dst_slot):
        base = (pl.program_id(0) * nt + step) * tile

        def body(j, carry):
            t = idx_ref[base + j]
            pltpu.make_async_copy(z_hbm.at[pl.ds(t, 1)], zbuf.at[dst_slot, pl.ds(j, 1)], gsem.at[dst_slot]).start()
            return carry

        lax.fori_loop(0, tile, body, 0, unroll=8)

    def wait_gather(src_slot):
        pltpu.make_async_copy(z_hbm.at[pl.ds(0, tile)], zbuf.at[src_slot], gsem.at[src_slot]).wait()

    def wait_scatter(src_slot):
        pltpu.make_async_copy(obuf.at[src_slot], z_hbm.at[pl.ds(0, tile), pl.ds(0, d)], ssem.at[src_slot]).wait()

    @pl.when(s == 0)
    def _():
        issue_gather(s, slot)

    wait_gather(slot)

    @pl.when(s + 1 < nt)
    def _():
        issue_gather(s + 1, 1 - slot)

    @pl.when(s >= 2)
    def _():
        wait_scatter(slot)

    x = zbuf[slot, :, d:].astype(BF16)
    ye = None
    for c0 in range(0, D_FF, ff_chunk):
        hg = jnp.dot(x, wg_ref[:, c0:c0 + ff_chunk], preferred_element_type=F32)
        hu = jnp.dot(x, wu_ref[:, c0:c0 + ff_chunk], preferred_element_type=F32)
        hid = (hg * jax.nn.sigmoid(hg) * hu).astype(BF16)
        part = jnp.dot(hid, wd_ref[c0:c0 + ff_chunk, :], preferred_element_type=F32)
        ye = part if ye is None else ye + part
    obuf[slot] = zbuf[slot, :, :d] + ye * gate_ref[...]

    base = (pl.program_id(0) * nt + s) * tile

    def scatter(j, carry):
        t = idx_ref[base + j]
        pltpu.make_async_copy(obuf.at[slot, pl.ds(j, 1)], z_hbm.at[pl.ds(t, 1), pl.ds(0, d)], ssem.at[slot]).start()
        return carry

    lax.fori_loop(0, tile, scatter, 0, unroll=8)

    @pl.when(s == nt - 1)
    def _():
        @pl.when(s >= 1)
        def _():
            wait_scatter(1 - slot)

        wait_scatter(slot)


def _moe(idx_flat, gate_col, z, wg, wu, wd, cap):
    n, d2 = z.shape
    d = d2 // 2
    tile = _pick(cap, MOE_TILE)
    grid_spec = pltpu.PrefetchScalarGridSpec(
        num_scalar_prefetch=1,
        grid=(N_EXPERTS, cap // tile),
        in_specs=[
            pl.BlockSpec((None, tile, 1), lambda e, s, idx: (e, s, 0)),
            pl.BlockSpec(memory_space=pl.ANY),
            pl.BlockSpec((None, d, D_FF), lambda e, s, idx: (e, 0, 0)),
            pl.BlockSpec((None, d, D_FF), lambda e, s, idx: (e, 0, 0)),
            pl.BlockSpec((None, D_FF, d), lambda e, s, idx: (e, 0, 0)),
        ],
        out_specs=pl.BlockSpec(memory_space=pl.ANY),
        scratch_shapes=[pltpu.VMEM((2, tile, d2), F32), pltpu.VMEM((2, tile, d), F32),
                        pltpu.SemaphoreType.DMA((2,)), pltpu.SemaphoreType.DMA((2,))],
    )
    return pl.pallas_call(
        functools.partial(_moe_kernel, tile=tile, ff_chunk=512),
        out_shape=jax.ShapeDtypeStruct((n, d2), F32),
        grid_spec=grid_spec,
        input_output_aliases={2: 0},
        compiler_params=_cparams(2),
        name="moe",
    )(idx_flat, gate_col, z, wg, wu, wd)


def _pe_kernel(x_ref, pe_ref, wpg_ref, wpp_ref, gfin_ref, o_ref, *, last):
    x = x_ref[...]
    gate = jax.nn.sigmoid(jnp.dot(x.astype(BF16), wpg_ref[...], preferred_element_type=F32))
    x = x + gate * jnp.dot(pe_ref[...].astype(BF16), wpp_ref[...], preferred_element_type=F32)
    if last:
        x = _rms(x) * gfin_ref[...]
    o_ref[...] = x


def _pe(z, pe2d, wpg, wpp, gfin, last):
    n, d = z.shape[0], z.shape[1] // 2
    tm = _pick(n, 512)
    return pl.pallas_call(
        functools.partial(_pe_kernel, last=last),
        out_shape=jax.ShapeDtypeStruct((n, d), F32),
        grid=(n // tm,),
        in_specs=[pl.BlockSpec((tm, d), lambda i: (i, 0)), pl.BlockSpec((tm, P_DIM), lambda i: (i, 0)),
                  _const_spec(wpg.shape), _const_spec(wpp.shape), _const_spec(gfin.shape)],
        out_specs=pl.BlockSpec((tm, d), lambda i: (i, 0)),
        compiler_params=_cparams(1),
        name="pe",
    )(z, pe2d, wpg, wpp, gfin)


def _pack_weights(l, w_in, w_alpha_up, b_alpha_up, norm_c, w_branch, w_router, b_router):
    widths = (512, 512, 512, 512, 128, 128, 256, 256, 512, 2 * GATE_RANK, 512)
    offs = [0]
    for w in widths:
        offs.append(offs[-1] + w)
    wqa, wka, wva, wqb, wkb, wvb, wqc, wkc, wvc, wac, wrc = (w_in[l][:, offs[i]:offs[i + 1]] for i in range(11))
    zeros64 = jnp.zeros((D_MODEL, DH_B), F32)
    qb_slots = []
    for h in range(H_B):
        wh = wqb[:, h * DH_B:(h + 1) * DH_B]
        qb_slots += [wh, zeros64] if h < GQ_B else [zeros64, wh]
    dup = lambda w: jnp.concatenate([jnp.concatenate([w[:, h * DK_C:(h + 1) * DK_C]] * 2, axis=1) for h in range(H_C)], axis=1)
    w_cat = jnp.concatenate(
        [wqa, wka, wva, jnp.concatenate(qb_slots, axis=1), wkb, wvb, dup(wqc), dup(wkc), wvc,
         jnp.pad(wac, ((0, 0), (0, LANES - 2 * GATE_RANK))), wrc], axis=1).astype(BF16)
    wbd = jnp.zeros((H_C, LANES, LANES), F32)
    bup = []
    for h in range(H_C):
        sl = slice(h * DK_C, (h + 1) * DK_C)
        wbd = wbd.at[h, 0:GATE_RANK, 0:DK_C].set(w_alpha_up[l, 0][:, sl])
        wbd = wbd.at[h, GATE_RANK:2 * GATE_RANK, DK_C:].set(w_alpha_up[l, 1][:, sl])
        bup.append(jnp.concatenate([b_alpha_up[l, 0, sl], b_alpha_up[l, 1, sl]])[None, :])
    bup = jnp.stack(bup)
    ng = norm_c[l][:, None, :]
    order = [h for g in range(GQ_B) for h in (g, GQ_B + g)]
    wb1 = jnp.concatenate([w_branch[l, 1][h * DH_B:(h + 1) * DH_B] for h in order], axis=0)
    wbr = jnp.stack([w_branch[l, 0], wb1, w_branch[l, 2]]).astype(BF16)
    wr = jnp.pad(w_router[l], ((0, 0), (0, LANES - N_EXPERTS))).astype(BF16)
    br = jnp.pad(b_router[l], (0, LANES - N_EXPERTS))[None, :]
    return w_cat, wvc.T.astype(BF16), wbd.astype(BF16), bup, ng, wbr, wr, br


def _trunk(x, pe, rel_bias, g_mix, lam_a, subln_a, sink_b, w_merge_gate, w_out, g_ffn, w_exp_gate, w_exp_up,
           w_exp_down, w_pe_proj, w_pe_gate, g_final, packed):
    bsz, s_len, d = x.shape
    n = bsz * s_len
    cap = EC_FACTOR * n // N_EXPERTS
    bias_a, bias_b = rel_bias[:, :H_A], rel_bias[:, H_A:]
    x2d = x.reshape(n, d)
    for l in range(DEPTH):
        w_cat, w_vct, wbd, bup, ng, wbr, wr, br = packed[l]
        lam_init = 0.8 - 0.6 * math.exp(-0.3 * l)
        *outs, vct = _inproj(x2d, g_mix[l][None, :], w_cat, w_vct)
        qa, ka, va, qb, kvb, qc, kc, vc, ac, rc = (o.reshape(bsz, s_len, o.shape[-1]) for o in outs)
        oa = _diffattn(qa, ka, va, lam_a[l], subln_a[l][None, :], bias_a, lam_init)
        ob = _winattn(qb, kvb, sink_b[l], bias_b)
        oc = _gla(qc, kc, vc, vct, ac, rc, wbd, bup, ng)
        z, aff = _merge(x2d, oa.reshape(n, -1), ob.reshape(n, -1), oc.reshape(n, -1), g_mix[l][None, :],
                        w_merge_gate[l], wbr, w_out[l], g_ffn[l][None, :], wr, br)
        aff3 = aff[:, :N_EXPERTS].T.reshape(N_EXPERTS, n // LANES, LANES)
        idx, gates = _select(aff3, cap)
        z = _moe(idx.reshape(-1), gates.reshape(N_EXPERTS, cap, 1), z,
                 w_exp_gate[l], w_exp_up[l], w_exp_down[l], cap)
        x2d = _pe(z, pe[l].reshape(n, -1), w_pe_gate[l], w_pe_proj[l], g_final[None, :], l == DEPTH - 1)
    return x2d.reshape(bsz, s_len, d)


def kernel(x_prompt, x_sample, p_prompt, p_sample, rel_bias, g_mix, w_in, lam_a, subln_a, sink_b, w_alpha_up,
           b_alpha_up, norm_c, w_branch, w_merge_gate, w_out, g_ffn, w_router, b_router, w_exp_gate, w_exp_up,
           w_exp_down, w_pe_proj, w_pe_gate, g_final):
    packed = [_pack_weights(l, w_in, w_alpha_up, b_alpha_up, norm_c, w_branch, w_router, b_router)
              for l in range(DEPTH)]
    shared = (rel_bias, g_mix, lam_a, subln_a, sink_b, w_merge_gate.astype(BF16), w_out.astype(BF16), g_ffn,
              w_exp_gate.astype(BF16), w_exp_up.astype(BF16), w_exp_down.astype(BF16), w_pe_proj.astype(BF16),
              w_pe_gate.astype(BF16), g_final, packed)
    y_prompt = _trunk(x_prompt, p_prompt, *shared)
    y_sample = _trunk(x_sample, p_sample, *shared)
    return (y_prompt, y_sample)
```

```python
import functools
import math

import jax
import jax.numpy as jnp
from jax import lax
from jax.experimental import pallas as pl
from jax.experimental.pallas import tpu as pltpu

F32 = jnp.float32
BF16 = jnp.bfloat16

D_MODEL = 1024
DEPTH = 2
H_A, DK_A, DV_A = 4, 64, 128
H_B, KV_B, GQ_B, DH_B = 8, 2, 4, 64
WINDOW, BLOCK_B = 128, 128
H_C, DK_C, DV_C = 4, 64, 128
GATE_RANK, GATE_TAU, CHUNK = 16, 16, 64
BRANCH_W, N_BRANCH = 512, 3
N_BUCKETS, MAX_DIST = 32, 128
N_EXPERTS, EC_FACTOR, D_FF = 16, 2, 2048
P_DIM = 256
NORM_EPS = 1e-6
LOG2E = math.log2(math.e)
LANES = 128
ROWS_PER_TILE = 8
VMEM_LIMIT = 56 * 1024 * 1024
MOE_TILE = 512

_IN_COLS = (
    ("qa", 512, BF16, DK_A ** -0.5 * LOG2E), ("ka", 512, BF16, 1.0), ("va", 512, BF16, 1.0),
    ("qb", 1024, BF16, DH_B ** -0.5), ("kvb", 256, BF16, 1.0),
    ("qc", 512, F32, DK_C ** -0.5), ("kc", 512, F32, 1.0), ("vc", 512, BF16, 1.0),
    ("ac", 128, F32, 1.0), ("rc", 512, BF16, 1.0),
)


def _cparams(n_axes):
    return pltpu.CompilerParams(dimension_semantics=("arbitrary",) * n_axes,
                                vmem_limit_bytes=VMEM_LIMIT)


def _const_spec(shape):
    nd = len(shape)
    return pl.BlockSpec(shape, lambda *_: (0,) * nd, pipeline_mode=pl.Buffered(1))


def _pick(n, pref):
    t = min(n, pref)
    while n % t:
        t //= 2
    return t


def _rms(x):
    return x * lax.rsqrt(jnp.mean(x * x, axis=-1, keepdims=True) + NORM_EPS)


def _t5_bucket(rel):
    half = N_BUCKETS // 2
    max_exact = half // 2
    ret = jnp.where(rel > 0, half, 0)
    n = jnp.abs(rel)
    nf = jnp.maximum(n, 1).astype(F32)
    large = max_exact + (jnp.log(nf / max_exact) / math.log(MAX_DIST / max_exact)
                         * (half - max_exact)).astype(jnp.int32)
    large = jnp.minimum(large, half - 1)
    return ret + jnp.where(n < max_exact, n, large)


def _bias_lookup(table, rel):
    onehot = (_t5_bucket(rel)[..., None] == jnp.arange(N_BUCKETS)).astype(F32)
    return jnp.einsum("...b,bh->...h", onehot, table, precision=lax.Precision.HIGHEST)


def _inproj_kernel(x_ref, g_ref, w_ref, wvt_ref, *out_refs):
    h = (_rms(x_ref[...]) * g_ref[...]).astype(BF16)
    c0 = 0
    for o_ref, (_, width, _, scale) in zip(out_refs, _IN_COLS):
        acc = jnp.dot(h, w_ref[:, c0:c0 + width], preferred_element_type=F32)
        if scale != 1.0:
            acc = acc * scale
        o_ref[...] = acc.astype(o_ref.dtype)
        c0 += width
    vt = lax.dot_general(wvt_ref[...], h, (((1,), (1,)), ((), ())), preferred_element_type=F32)
    out_refs[-1][...] = vt.astype(out_refs[-1].dtype)


def _inproj(x2d, g, w_cat, w_vct):
    n, d = x2d.shape
    tm = _pick(n, 512)
    out_shape = [jax.ShapeDtypeStruct((n, w), dt) for _, w, dt, _ in _IN_COLS]
    out_specs = [pl.BlockSpec((tm, w), lambda i: (i, 0)) for _, w, _, _ in _IN_COLS]
    out_shape.append(jax.ShapeDtypeStruct((w_vct.shape[0], n), BF16))
    out_specs.append(pl.BlockSpec((w_vct.shape[0], tm), lambda i: (0, i)))
    return pl.pallas_call(
        _inproj_kernel,
        out_shape=out_shape,
        grid=(n // tm,),
        in_specs=[pl.BlockSpec((tm, d), lambda i: (i, 0)), _const_spec(g.shape), _const_spec(w_cat.shape),
                  _const_spec(w_vct.shape)],
        out_specs=out_specs,
        compiler_params=_cparams(1),
        name="inproj",
    )(x2d, g, w_cat, w_vct)


def _diffattn_kernel(lam_ref, subg_ref, cst_ref, band_ref, q_ref, k_ref, v_ref, o_ref, s_ref, *, lam_init):
    i = pl.program_id(2)
    nq = pl.num_programs(2)
    tq = q_ref.shape[0]
    s_len = k_ref.shape[0]
    lp = lam_ref[...]
    lam = (jnp.exp(jnp.sum(lp[0:1] * lp[1:2], axis=-1, keepdims=True))
           - jnp.exp(jnp.sum(lp[2:3] * lp[3:4], axis=-1, keepdims=True))) + lam_init
    q = q_ref[...]
    k = k_ref[...]
    lane = lax.broadcasted_iota(jnp.int32, (1, LANES), 1)
    zero = jnp.zeros_like(q)
    col = lax.broadcasted_iota(jnp.int32, (1, s_len), 1)
    base = jnp.where(col < i * tq, cst_ref[0:1, 0:1], cst_ref[1:2, 0:1])
    for m in range(2):
        qm = jnp.where((lane < DK_A) == (m == 0), q, zero)
        s_ref[m] = lax.dot_general(qm, k, (((1,), (1,)), ((), ())), preferred_element_type=F32) + base
    for d in (-1, 0, 1):
        kb = i + d

        @pl.when((kb >= 0) & (kb < nq))
        def _(d=d, kb=kb):
            off = pl.multiple_of(kb * tq, tq)
            delta = band_ref[d + 1]
            for m in range(2):
                s_ref[m, :, pl.ds(off, tq)] += delta

    probs, inv = [], []
    for m in range(2):
        s = s_ref[m]
        p = jnp.exp2(s - jnp.max(s, axis=-1, keepdims=True))
        probs.append(p)
        inv.append(1.0 / jnp.sum(p, axis=-1, keepdims=True))
    w = (probs[0] * inv[0] - probs[1] * (lam * inv[1])).astype(BF16)
    o = jnp.dot(w, v_ref[...], preferred_element_type=F32)
    o_ref[...] = (_rms(o) * subg_ref[...] * (1.0 - lam_init)).astype(o_ref.dtype)


def _diffattn(qa, ka, va, lam_l, subg_l, bias_a, lam_init):
    bsz, s_len, _ = qa.shape
    tq = _pick(s_len, 256)
    nq = s_len // tq
    ii = jnp.arange(tq)
    rel = jnp.stack([d * tq + ii[None, :] - ii[:, None] for d in (-1, 0, 1)])
    bias_a = bias_a * LOG2E
    table = _bias_lookup(bias_a, rel)
    c_left = bias_a[N_BUCKETS // 2 - 1]
    c_right = bias_a[N_BUCKETS - 1]
    side = jnp.stack([c_left, c_right, c_right])
    band = (table - side[:, None, None, :]).transpose(3, 0, 1, 2)
    cst = jnp.broadcast_to(jnp.stack([c_left, c_right], axis=1)[:, :, None], (H_A, 2, LANES))
    cst = jnp.pad(cst, ((0, 0), (0, 6), (0, 0)))
    return pl.pallas_call(
        functools.partial(_diffattn_kernel, lam_init=lam_init),
        out_shape=jax.ShapeDtypeStruct((bsz, s_len, H_A * DV_A), BF16),
        grid=(bsz, H_A, nq),
        in_specs=[
            _const_spec(lam_l.shape),
            _const_spec(subg_l.shape),
            pl.BlockSpec((None, 8, LANES), lambda b, h, i: (h, 0, 0)),
            pl.BlockSpec((None, 3, tq, tq), lambda b, h, i: (h, 0, 0, 0)),
            pl.BlockSpec((None, tq, LANES), lambda b, h, i: (b, i, h)),
            pl.BlockSpec((None, s_len, LANES), lambda b, h, i: (b, 0, h)),
            pl.BlockSpec((None, s_len, LANES), lambda b, h, i: (b, 0, h)),
        ],
        out_specs=pl.BlockSpec((None, tq, LANES), lambda b, h, i: (b, i, h)),
        scratch_shapes=[pltpu.VMEM((2, tq, s_len), F32)],
        compiler_params=_cparams(3),
        name="diffattn",
    )(lam_l, subg_l, cst, band, qa, ka, va)


def _winattn_kernel(sink_ref, band_ref, q_ref, kp_ref, kc_ref, kn_ref, o_ref, *, s_len):
    i = pl.program_id(1)
    tq = q_ref.shape[0]
    kv = jnp.concatenate([kp_ref[...], kc_ref[...], kn_ref[...]], axis=0)
    kk = kv[:, :LANES]
    vv = kv[:, LANES:]
    kabs = i * tq - BLOCK_B + lax.broadcasted_iota(jnp.int32, (1, tq + 2 * BLOCK_B), 1)
    in_seq = (kabs >= 0) & (kabs < s_len)
    lane = lax.broadcasted_iota(jnp.int32, (1, LANES), 1)
    row = lax.broadcasted_iota(jnp.int32, (GQ_B * tq, 1), 0)
    outs = []
    for c in range(KV_B):
        qs = jnp.concatenate([q_ref[:, h * LANES:(h + 1) * LANES] for h in range(c * GQ_B, (c + 1) * GQ_B)], axis=0)
        s = lax.dot_general(qs, kk, (((1,), (1,)), ((), ())), preferred_element_type=F32)
        s = jnp.where(in_seq, s + band_ref[c], -jnp.inf)
        sk = sink_ref[c * GQ_B:c * GQ_B + 1, 0:1]
        for g in range(1, GQ_B):
            sk = jnp.where(row >= g * tq, sink_ref[c * GQ_B + g:c * GQ_B + g + 1, 0:1], sk)
        m = jnp.maximum(jnp.max(s, axis=-1, keepdims=True), sk)
        p = jnp.exp(s - m)
        inv = 1.0 / (jnp.sum(p, axis=-1, keepdims=True) + jnp.exp(sk - m))
        outs.append(jnp.dot(p.astype(BF16), vv, preferred_element_type=F32) * inv)
    for g in range(GQ_B):
        o_ref[:, g * LANES:(g + 1) * LANES] = jnp.where(
            lane < DH_B, outs[0][g * tq:(g + 1) * tq], outs[1][g * tq:(g + 1) * tq]).astype(o_ref.dtype)


def _winattn(qb, kvb, sink_l, bias_b):
    bsz, s_len, _ = qb.shape
    nb = s_len // BLOCK_B
    tq = _pick(s_len, 512)
    r = tq // BLOCK_B
    rel = jnp.arange(tq + 2 * BLOCK_B)[None, :] - BLOCK_B - jnp.arange(tq)[:, None]
    band = jnp.where((jnp.abs(rel) <= WINDOW)[:, :, None], _bias_lookup(bias_b, rel), -jnp.inf)
    band = band.transpose(2, 0, 1).reshape(KV_B, GQ_B * tq, tq + 2 * BLOCK_B)
    sink = jnp.broadcast_to(sink_l[:, None], (H_B, LANES))
    kvw = 2 * KV_B * DH_B
    return pl.pallas_call(
        functools.partial(_winattn_kernel, s_len=s_len),
        out_shape=jax.ShapeDtypeStruct((bsz, s_len, H_B * DH_B), BF16),
        grid=(bsz, s_len // tq),
        in_specs=[
            _const_spec(sink.shape),
            _const_spec(band.shape),
            pl.BlockSpec((None, tq, H_B * LANES), lambda b, i: (b, i, 0)),
            pl.BlockSpec((None, BLOCK_B, kvw), lambda b, i: (b, jnp.maximum(i * r - 1, 0), 0)),
            pl.BlockSpec((None, tq, kvw), lambda b, i: (b, i, 0)),
            pl.BlockSpec((None, BLOCK_B, kvw), lambda b, i: (b, jnp.minimum((i + 1) * r, nb - 1), 0)),
        ],
        out_specs=pl.BlockSpec((None, tq, H_B * DH_B), lambda b, i: (b, i, 0)),
        compiler_params=_cparams(2),
        name="winattn",
    )(sink, band, qb, kvb, kvb, kvb)


def _gla_kernel(q_ref, k_ref, v_ref, vt_ref, a_ref, r_ref, wbd_ref, bup_ref, ng_ref, o_ref,
                g_scr, acc_scr, qd_scr, kv_scr, dec_scr, tot_scr):
    s_len = q_ref.shape[0]
    nc = s_len // CHUNK
    pair = 2 * CHUNK
    pre = jnp.dot(a_ref[...].astype(BF16), wbd_ref[...], preferred_element_type=F32) + bup_ref[...]
    g_scr[...] = (jnp.minimum(pre, 0.0) - jnp.log1p(jnp.exp(-jnp.abs(pre)))) * (1.0 / GATE_TAU)
    lane = lax.broadcasted_iota(jnp.int32, (1, LANES), 1)
    fwd_lane = lane < DK_C
    first = lax.broadcasted_iota(jnp.int32, (pair, 1), 0) < CHUNK
    ri = lax.broadcasted_iota(jnp.int32, (2 * pair, pair), 0)
    ci = lax.broadcasted_iota(jnp.int32, (2 * pair, pair), 1)
    sh = CHUNK.bit_length() - 1
    blk, rr, cc = ri >> sh, ri & (CHUNK - 1), ci & (CHUNK - 1)
    same = (ci >> sh) == (blk & 1)
    cum = jnp.where(same & (((blk < 2) & (cc <= rr)) | ((blk >= 2) & (cc >= rr))), 1.0, 0.0).astype(BF16)
    qrow = lax.broadcasted_iota(jnp.int32, (pair, LANES), 0) & (CHUNK - 1)
    kcol = lax.broadcasted_iota(jnp.int32, (pair, LANES), 1)
    keep = ((kcol < CHUNK) & (kcol <= qrow)) | ((kcol >= CHUNK) & (kcol - CHUNK >= qrow))
    nt = (((1,), (1,)), ((), ()))

    def cumsum(p, carry):
        off = pl.multiple_of(p * pair, pair)
        g = g_scr[pl.ds(off, pair), :]
        g1 = g.astype(BF16)
        r1 = g - g1.astype(F32)
        g2 = r1.astype(BF16)
        g3 = (r1 - g2.astype(F32)).astype(BF16)
        b3 = jnp.dot(cum, jnp.concatenate([g1, g2, g3], axis=1), preferred_element_type=F32)
        bb = b3[:, :LANES] + b3[:, LANES:2 * LANES] + b3[:, 2 * LANES:]
        g_scr[pl.ds(off, pair), :] = jnp.where(fwd_lane, bb[:pair], bb[pair:])
        tot_scr[2 * p] = jnp.where(fwd_lane, bb[CHUNK - 1:CHUNK], bb[pair:pair + 1])
        tot_scr[2 * p + 1] = jnp.where(fwd_lane, bb[pair - 1:pair], bb[pair + CHUNK:pair + CHUNK + 1])
        return carry

    lax.fori_loop(0, nc // 2, cumsum, 0, unroll=2)

    def phase1(p, carry):
        off = pl.multiple_of(p * pair, pair)
        b = g_scr[pl.ds(off, pair), :]
        t0 = tot_scr[2 * p]
        t1 = tot_scr[2 * p + 1]
        q = q_ref[pl.ds(off, pair), :]
        k = k_ref[pl.ds(off, pair), :]
        v = v_ref[pl.ds(off, pair), :]
        qd = q * jnp.exp(b)
        kd = k * jnp.exp(-b)
        ki = (k * jnp.exp(jnp.where(first, t0, t1) - b)).astype(BF16)
        qd_f = jnp.where(fwd_lane, qd, 0.0).astype(BF16)
        qd_b = jnp.where(fwd_lane, 0.0, qd).astype(BF16)
        qd_scr[2 * p] = jnp.concatenate([qd_f[:CHUNK], qd_b[:CHUNK]], axis=0)
        qd_scr[2 * p + 1] = jnp.concatenate([qd_f[CHUNK:], qd_b[CHUNK:]], axis=0)
        kd_f = jnp.where(fwd_lane, kd, 0.0).astype(BF16)
        kd_b = jnp.where(fwd_lane, 0.0, kd).astype(BF16)
        kd4 = jnp.concatenate([kd_f[:CHUNK], kd_b[:CHUNK], kd_f[CHUNK:], kd_b[CHUNK:]], axis=0)
        sc = lax.dot_general(qd.astype(BF16), kd4, nt, preferred_element_type=F32)
        a = jnp.where(keep, jnp.where(first, sc[:, :LANES], sc[:, LANES:]), 0.0).astype(BF16)
        zero = jnp.zeros_like(a)
        a4 = jnp.concatenate([jnp.where(first, a, zero), jnp.where(first, zero, a)], axis=1)
        v4 = jnp.concatenate([v[:CHUNK], v[:CHUNK], v[CHUNK:], v[CHUNK:]], axis=0)
        acc_scr[pl.ds(off, pair), :] = jnp.dot(a4, v4, preferred_element_type=F32)
        vt = vt_ref[:, pl.ds(off, pair)]
        vzero = jnp.zeros_like(vt)
        kv_scr[2 * p] = jnp.dot(jnp.where(fwd_lane, vt, vzero), ki, preferred_element_type=F32)
        kv_scr[2 * p + 1] = jnp.dot(jnp.where(fwd_lane, vzero, vt), ki, preferred_element_type=F32)
        dec_scr[2 * p] = jnp.exp(t0)
        dec_scr[2 * p + 1] = jnp.exp(t1)
        return carry

    lax.fori_loop(0, nc // 2, phase1, 0, unroll=4)

    def phase2(t, state):
        cf = t
        cb = nc - 1 - t
        lhs = jnp.concatenate([qd_scr[cf, 0:CHUNK, :], qd_scr[cb, CHUNK:2 * CHUNK, :]], axis=0)
        o2 = lax.dot_general(lhs, state.astype(BF16), nt, preferred_element_type=F32)
        acc_scr[pl.ds(pl.multiple_of(cf * CHUNK, CHUNK), CHUNK), :] += o2[:CHUNK]
        acc_scr[pl.ds(pl.multiple_of(cb * CHUNK, CHUNK), CHUNK), :] += o2[CHUNK:]
        dec = jnp.where(fwd_lane, dec_scr[cf], dec_scr[cb])
        inc = jnp.where(fwd_lane, kv_scr[cf], kv_scr[cb])
        return dec * state + inc

    lax.fori_loop(0, nc, phase2, jnp.zeros((DV_C, LANES), F32), unroll=4)
    r = r_ref[...].astype(F32)
    o_ref[...] = (_rms(acc_scr[...]) * ng_ref[...] * (r * jax.nn.sigmoid(r))).astype(o_ref.dtype)


def _gla(qc, kc, vc, vct, ac, rc, wbd, bup, ng):
    bsz, s_len, _ = qc.shape
    nc = s_len // CHUNK
    blk = lambda: pl.BlockSpec((None, s_len, LANES), lambda b, h: (b, 0, h))
    per_head = lambda r: pl.BlockSpec((None, r, LANES), lambda b, h: (h, 0, 0))
    return pl.pallas_call(
        _gla_kernel,
        out_shape=jax.ShapeDtypeStruct((bsz, s_len, H_C * DV_C), BF16),
        grid=(bsz, H_C),
        in_specs=[blk(), blk(), blk(),
                  pl.BlockSpec((DV_C, s_len), lambda b, h: (h, b)),
                  pl.BlockSpec((None, s_len, LANES), lambda b, h: (b, 0, 0)),
                  blk(), per_head(LANES), per_head(1), per_head(1)],
        out_specs=blk(),
        scratch_shapes=[pltpu.VMEM((s_len, LANES), F32), pltpu.VMEM((s_len, LANES), F32),
                        pltpu.VMEM((nc, 2 * CHUNK, LANES), BF16), pltpu.VMEM((nc, DV_C, LANES), F32),
                        pltpu.VMEM((nc, 1, LANES), F32), pltpu.VMEM((nc, 1, LANES), F32)],
        compiler_params=_cparams(2),
        name="gla",
    )(qc, kc, vc, vct, ac, rc, wbd, bup, ng)


def _merge_kernel(x_ref, oa_ref, ob_ref, oc_ref, gmix_ref, wmg_ref, wbr_ref, wout_ref, gffn_ref, wr_ref, br_ref,
                  z_ref, aff_ref):
    x = x_ref[...]
    d = x.shape[-1]
    h = (_rms(x) * gmix_ref[...]).astype(BF16)
    acc = None
    for g, o_ref in enumerate((oa_ref, ob_ref, oc_ref)):
        gate = jax.nn.sigmoid(jnp.dot(h, wmg_ref[:, g * D_MODEL:(g + 1) * D_MODEL], preferred_element_type=F32))
        br = jnp.dot(o_ref[...], wbr_ref[g], preferred_element_type=F32)
        acc = gate * br if acc is None else acc + gate * br
    x1 = x + jnp.dot(acc.astype(BF16), wout_ref[...], preferred_element_type=F32)
    xt = _rms(x1) * gffn_ref[...]
    z_ref[:, :d] = x1
    z_ref[:, d:] = xt
    logits = jnp.dot(xt.astype(BF16), wr_ref[...], preferred_element_type=F32) + br_ref[...]
    lane = lax.broadcasted_iota(jnp.int32, logits.shape, 1)
    logits = jnp.where(lane < N_EXPERTS, logits, -jnp.inf)
    e = jnp.exp(logits - jnp.max(logits, axis=-1, keepdims=True))
    aff_ref[...] = (e / jnp.sum(e, axis=-1, keepdims=True)).T


def _merge(x2d, oa, ob, oc, gmix, wmg, wbr, wout, gffn, wr, br):
    n, d = x2d.shape
    tm = _pick(n, 512)
    row = lambda w: pl.BlockSpec((tm, w), lambda i: (i, 0))
    return pl.pallas_call(
        _merge_kernel,
        out_shape=[jax.ShapeDtypeStruct((n, 2 * d), F32), jax.ShapeDtypeStruct((LANES, n), F32)],
        grid=(n // tm,),
        in_specs=[row(d), row(BRANCH_W), row(BRANCH_W), row(BRANCH_W)]
        + [_const_spec(a.shape) for a in (gmix, wmg, wbr, wout, gffn, wr, br)],
        out_specs=[row(2 * d), pl.BlockSpec((LANES, tm), lambda i: (0, i))],
        compiler_params=_cparams(1),
        name="merge",
    )(x2d, oa, ob, oc, gmix, wmg, wbr, wout, gffn, wr, br)


def _prefix_counts(m):
    r = m.shape[0]
    li = lax.broadcasted_iota(jnp.int32, (LANES, LANES), 0)
    lj = lax.broadcasted_iota(jnp.int32, (LANES, LANES), 1)
    within = jnp.dot(m.astype(BF16), (li <= lj).astype(BF16), preferred_element_type=F32)
    rowtot = jnp.broadcast_to(within[:, LANES - 1:LANES], (r, LANES))
    ri = lax.broadcasted_iota(jnp.int32, (r, r), 0)
    rj = lax.broadcasted_iota(jnp.int32, (r, r), 1)
    before = jnp.dot((rj < ri).astype(F32), rowtot, preferred_element_type=F32, precision=lax.Precision.HIGHEST)
    return within, rowtot, before


def _select_kernel(aff_ref, idx_ref, gate_ref, *, cap):
    aff = aff_ref[...]
    r = aff.shape[0]
    bits = pltpu.bitcast(aff, jnp.int32)

    def bit_step(t, thr):
        cand = thr | jnp.left_shift(jnp.int32(1), 30 - t)
        cnt = jnp.sum((bits >= cand).astype(jnp.int32), keepdims=True)
        return jnp.where(cnt >= cap, cand, thr)

    thr = lax.fori_loop(0, 31, bit_step, jnp.zeros((1, 1), jnp.int32))
    gt = bits > thr
    eq = bits == thr
    need = (cap - jnp.sum(gt.astype(jnp.int32), keepdims=True)).astype(F32)
    eqf = eq.astype(F32)
    w_eq, _, b_eq = _prefix_counts(eqf)
    sel = gt | (eq & ((w_eq - eqf + b_eq) < need))
    within, rowtot, before = _prefix_counts(sel.astype(F32))
    row_incl = before[:, 0:1] + rowtot[:, 0:1]
    slot = lax.broadcasted_iota(jnp.int32, (1, cap), 1).astype(F32)
    done = row_incl <= slot
    r_j = jnp.sum(done.astype(F32), axis=0, keepdims=True)
    off_j = jnp.sum(jnp.where(done, rowtot[:, 0:1], 0.0), axis=0, keepdims=True)
    onehot = lax.broadcasted_iota(jnp.int32, (r, cap), 0).astype(F32) == r_j
    cnt_t = jnp.dot(within.T.astype(BF16), onehot.astype(BF16), preferred_element_type=F32)
    lane_j = jnp.sum((cnt_t <= (slot - off_j)).astype(F32), axis=0, keepdims=True)
    idx_ref[...] = (r_j * LANES + lane_j).astype(jnp.int32)
    aff_t = jnp.dot(aff.T, onehot.astype(F32), preferred_element_type=F32, precision=lax.Precision.HIGHEST)
    lsub = lax.broadcasted_iota(jnp.int32, (LANES, cap), 0).astype(F32)
    gate_ref[...] = jnp.sum(jnp.where(lsub == lane_j, aff_t, 0.0), axis=0, keepdims=True)


def _select(aff3, cap):
    e, r, _ = aff3.shape
    return pl.pallas_call(
        functools.partial(_select_kernel, cap=cap),
        out_shape=[jax.ShapeDtypeStruct((e, 1, cap), jnp.int32), jax.ShapeDtypeStruct((e, 1, cap), F32)],
        grid=(e,),
        in_specs=[pl.BlockSpec((None, r, LANES), lambda i: (i, 0, 0))],
        out_specs=[pl.BlockSpec((None, 1, cap), lambda i: (i, 0, 0)), pl.BlockSpec((None, 1, cap), lambda i: (i, 0, 0))],
        compiler_params=_cparams(1),
        name="select",
    )(aff3)


def _moe_kernel(idx_ref, gate_ref, z_in_hbm, wg_ref, wu_ref, wd_ref, z_hbm, zbuf, obuf, gsem, ssem, *,
                tile, ff_chunk, nt):
    del z_in_hbm
    e = pl.program_id(0)
    s = pl.program_id(1)
    d = obuf.shape[-1]
    slot = s % 2
    n_ff = D_FF // ff_chunk

    groups = tile // ROWS_PER_TILE
    sh = ROWS_PER_TILE.bit_length() - 1

    def issue(step, buf_slot, gather):
        base = (e * nt + step) * tile

        def body(g, carry):
            for k in range(ROWS_PER_TILE):
                t = idx_ref[base + g * ROWS_PER_TILE + k]
                hi, lo = t >> sh, t & (ROWS_PER_TILE - 1)
                if gather:
                    pltpu.make_async_copy(z_hbm.at[hi, pl.ds(lo, 1)], zbuf.at[buf_slot, g, pl.ds(k, 1)],
                                          gsem.at[buf_slot]).start()
                else:
                    pltpu.make_async_copy(obuf.at[buf_slot, g, pl.ds(k, 1)], z_hbm.at[hi, pl.ds(lo, 1), pl.ds(0, d)],
                                          ssem.at[buf_slot]).start()
            return carry

        lax.fori_loop(0, groups, body, 0)

    def wait_gather(src_slot):
        pltpu.make_async_copy(z_hbm.at[pl.ds(0, groups)], zbuf.at[src_slot], gsem.at[src_slot]).wait()

    def wait_scatter(src_slot):
        pltpu.make_async_copy(obuf.at[src_slot], z_hbm.at[pl.ds(0, groups), :, pl.ds(0, d)], ssem.at[src_slot]).wait()

    @pl.when(s == 0)
    def _():
        issue(s, slot, True)

    wait_gather(slot)

    @pl.when(s + 1 < nt)
    def _():
        issue(s + 1, 1 - slot, True)

    @pl.when(s >= 2)
    def _():
        wait_scatter(slot)

    zt = zbuf[slot].reshape(tile, 2 * d)
    x = zt[:, d:].astype(BF16)
    ye = None
    for c0 in range(0, D_FF, ff_chunk):
        hg = jnp.dot(x, wg_ref[:, c0:c0 + ff_chunk], preferred_element_type=F32)
        hu = jnp.dot(x, wu_ref[:, c0:c0 + ff_chunk], preferred_element_type=F32)
        hid = (hg * jax.nn.sigmoid(hg) * hu).astype(BF16)
        part = jnp.dot(hid, wd_ref[c0:c0 + ff_chunk, :], preferred_element_type=F32)
        ye = part if ye is None else ye + part
    obuf[slot] = (zt[:, :d] + ye * gate_ref[...]).reshape(groups, ROWS_PER_TILE, d)
    issue(s, slot, False)

    @pl.when(s == nt - 1)
    def _():
        if nt > 1:
            wait_scatter(1 - slot)
        wait_scatter(slot)


def _moe(idx_flat, gate_col, z, wg, wu, wd, cap):
    n, d2 = z.shape
    d = d2 // 2
    tile = _pick(cap, MOE_TILE)
    grid_spec = pltpu.PrefetchScalarGridSpec(
        num_scalar_prefetch=1,
        grid=(N_EXPERTS, cap // tile),
        in_specs=[
            pl.BlockSpec((None, tile, 1), lambda e, s, idx: (e, s, 0)),
            pl.BlockSpec(memory_space=pl.ANY),
            pl.BlockSpec((None, d, D_FF), lambda e, s, idx: (e, 0, 0)),
            pl.BlockSpec((None, d, D_FF), lambda e, s, idx: (e, 0, 0)),
            pl.BlockSpec((None, D_FF, d), lambda e, s, idx: (e, 0, 0)),
        ],
        out_specs=pl.BlockSpec(memory_space=pl.ANY),
        scratch_shapes=[pltpu.VMEM((2, tile // ROWS_PER_TILE, ROWS_PER_TILE, d2), F32),
                        pltpu.VMEM((2, tile // ROWS_PER_TILE, ROWS_PER_TILE, d), F32),
                        pltpu.SemaphoreType.DMA((2,)), pltpu.SemaphoreType.DMA((2,))],
    )
    z3 = z.reshape(n // ROWS_PER_TILE, ROWS_PER_TILE, d2)
    return pl.pallas_call(
        functools.partial(_moe_kernel, tile=tile, ff_chunk=512, nt=cap // tile),
        out_shape=jax.ShapeDtypeStruct(z3.shape, F32),
        grid_spec=grid_spec,
        input_output_aliases={2: 0},
        compiler_params=_cparams(2),
        name="moe",
    )(idx_flat, gate_col, z3, wg, wu, wd).reshape(n, d2)


def _pe_kernel(x_ref, pe_ref, wpg_ref, wpp_ref, gfin_ref, o_ref, *, last):
    x = x_ref[...]
    gate = jax.nn.sigmoid(jnp.dot(x.astype(BF16), wpg_ref[...], preferred_element_type=F32))
    x = x + gate * jnp.dot(pe_ref[...].astype(BF16), wpp_ref[...], preferred_element_type=F32)
    if last:
        x = _rms(x) * gfin_ref[...]
    o_ref[...] = x


def _pe(z, pe2d, wpg, wpp, gfin, last):
    n, d = z.shape[0], z.shape[1] // 2
    tm = _pick(n, 512)
    return pl.pallas_call(
        functools.partial(_pe_kernel, last=last),
        out_shape=jax.ShapeDtypeStruct((n, d), F32),
        grid=(n // tm,),
        in_specs=[pl.BlockSpec((tm, d), lambda i: (i, 0)), pl.BlockSpec((tm, P_DIM), lambda i: (i, 0)),
                  _const_spec(wpg.shape), _const_spec(wpp.shape), _const_spec(gfin.shape)],
        out_specs=pl.BlockSpec((tm, d), lambda i: (i, 0)),
        compiler_params=_cparams(1),
        name="pe",
    )(z, pe2d, wpg, wpp, gfin)


def _pack_weights(l, w_in, w_alpha_up, b_alpha_up, norm_c, w_branch, w_router, b_router):
    widths = (512, 512, 512, 512, 128, 128, 256, 256, 512, 2 * GATE_RANK, 512)
    offs = [0]
    for w in widths:
        offs.append(offs[-1] + w)
    wqa, wka, wva, wqb, wkb, wvb, wqc, wkc, wvc, wac, wrc = (w_in[l][:, offs[i]:offs[i + 1]] for i in range(11))
    zeros64 = jnp.zeros((D_MODEL, DH_B), F32)
    qb_slots = []
    for h in range(H_B):
        wh = wqb[:, h * DH_B:(h + 1) * DH_B]
        qb_slots += [wh, zeros64] if h < GQ_B else [zeros64, wh]
    dup = lambda w: jnp.concatenate([jnp.concatenate([w[:, h * DK_C:(h + 1) * DK_C]] * 2, axis=1) for h in range(H_C)], axis=1)
    w_cat = jnp.concatenate(
        [wqa, wka, wva, jnp.concatenate(qb_slots, axis=1), wkb, wvb, dup(wqc), dup(wkc), wvc,
         jnp.pad(wac, ((0, 0), (0, LANES - 2 * GATE_RANK))), wrc], axis=1).astype(BF16)
    wbd = jnp.zeros((H_C, LANES, LANES), F32)
    bup = []
    for h in range(H_C):
        sl = slice(h * DK_C, (h + 1) * DK_C)
        wbd = wbd.at[h, 0:GATE_RANK, 0:DK_C].set(w_alpha_up[l, 0][:, sl])
        wbd = wbd.at[h, GATE_RANK:2 * GATE_RANK, DK_C:].set(w_alpha_up[l, 1][:, sl])
        bup.append(jnp.concatenate([b_alpha_up[l, 0, sl], b_alpha_up[l, 1, sl]])[None, :])
    bup = jnp.stack(bup)
    ng = norm_c[l][:, None, :]
    order = [h for g in range(GQ_B) for h in (g, GQ_B + g)]
    wb1 = jnp.concatenate([w_branch[l, 1][h * DH_B:(h + 1) * DH_B] for h in order], axis=0)
    wbr = jnp.stack([w_branch[l, 0], wb1, w_branch[l, 2]]).astype(BF16)
    wr = jnp.pad(w_router[l], ((0, 0), (0, LANES - N_EXPERTS))).astype(BF16)
    br = jnp.pad(b_router[l], (0, LANES - N_EXPERTS))[None, :]
    return w_cat, wvc.T.astype(BF16), wbd.astype(BF16), bup, ng, wbr, wr, br


def _trunk(x, pe, rel_bias, g_mix, lam_a, subln_a, sink_b, w_merge_gate, w_out, g_ffn, w_exp_gate, w_exp_up,
           w_exp_down, w_pe_proj, w_pe_gate, g_final, packed):
    bsz, s_len, d = x.shape
    n = bsz * s_len
    cap = EC_FACTOR * n // N_EXPERTS
    bias_a, bias_b = rel_bias[:, :H_A], rel_bias[:, H_A:]
    x2d = x.reshape(n, d)
    for l in range(DEPTH):
        w_cat, w_vct, wbd, bup, ng, wbr, wr, br = packed[l]
        lam_init = 0.8 - 0.6 * math.exp(-0.3 * l)
        *outs, vct = _inproj(x2d, g_mix[l][None, :], w_cat, w_vct)
        qa, ka, va, qb, kvb, qc, kc, vc, ac, rc = (o.reshape(bsz, s_len, o.shape[-1]) for o in outs)
        oa = _diffattn(qa, ka, va, lam_a[l], subln_a[l][None, :], bias_a, lam_init)
        ob = _winattn(qb, kvb, sink_b[l], bias_b)
        oc = _gla(qc, kc, vc, vct, ac, rc, wbd, bup, ng)
        z, aff = _merge(x2d, oa.reshape(n, -1), ob.reshape(n, -1), oc.reshape(n, -1), g_mix[l][None, :],
                        w_merge_gate[l], wbr, w_out[l], g_ffn[l][None, :], wr, br)
        aff3 = aff[:N_EXPERTS].reshape(N_EXPERTS, n // LANES, LANES)
        idx, gates = _select(aff3, cap)
        z = _moe(idx.reshape(-1), gates.reshape(N_EXPERTS, cap, 1), z,
                 w_exp_gate[l], w_exp_up[l], w_exp_down[l], cap)
        x2d = _pe(z, pe[l].reshape(n, -1), w_pe_gate[l], w_pe_proj[l], g_final[None, :], l == DEPTH - 1)
    return x2d.reshape(bsz, s_len, d)


def kernel(x_prompt, x_sample, p_prompt, p_sample, rel_bias, g_mix, w_in, lam_a, subln_a, sink_b, w_alpha_up,
           b_alpha_up, norm_c, w_branch, w_merge_gate, w_out, g_ffn, w_router, b_router, w_exp_gate, w_exp_up,
           w_exp_down, w_pe_proj, w_pe_gate, g_final):
    packed = [_pack_weights(l, w_in, w_alpha_up, b_alpha_up, norm_c, w_branch, w_router, b_router)
              for l in range(DEPTH)]
    shared = (rel_bias, g_mix, lam_a, subln_a, sink_b, w_merge_gate.astype(BF16), w_out.astype(BF16), g_ffn,
              w_exp_gate.astype(BF16), w_exp_up.astype(BF16), w_exp_down.astype(BF16), w_pe_proj.astype(BF16),
              w_pe_gate.astype(BF16), g_final, packed)
    y_prompt = _trunk(x_prompt, p_prompt, *shared)
    y_sample = _trunk(x_sample, p_sample, *shared)
    return (y_prompt, y_sample)
```

```python
import functools
import math

import jax
import jax.numpy as jnp
from jax import lax
from jax.experimental import pallas as pl
from jax.experimental.pallas import tpu as pltpu

F32 = jnp.float32
BF16 = jnp.bfloat16

D_MODEL = 1024
DEPTH = 2
H_A, DK_A, DV_A = 4, 64, 128
H_B, KV_B, GQ_B, DH_B = 8, 2, 4, 64
WINDOW, BLOCK_B = 128, 128
H_C, DK_C, DV_C = 4, 64, 128
GATE_RANK, GATE_TAU, CHUNK = 16, 16, 64
BRANCH_W, N_BRANCH = 512, 3
N_BUCKETS, MAX_DIST = 32, 128
N_EXPERTS, EC_FACTOR, D_FF = 16, 2, 2048
P_DIM = 256
NORM_EPS = 1e-6
LOG2E = math.log2(math.e)
LANES = 128
ROWS_PER_TILE = 8
VMEM_LIMIT = 56 * 1024 * 1024
MOE_TILE = 512

_IN_COLS = (
    ("qa", 512, BF16, DK_A ** -0.5 * LOG2E), ("ka", 512, BF16, 1.0), ("va", 512, BF16, 1.0),
    ("qb", 1024, BF16, DH_B ** -0.5 * LOG2E), ("kvb", 256, BF16, 1.0),
    ("qc", 512, F32, DK_C ** -0.5), ("kc", 512, F32, 1.0), ("vc", 512, BF16, 1.0),
    ("ac", 128, F32, 1.0), ("rc", 512, BF16, 1.0),
)


def _cparams(n_axes):
    return pltpu.CompilerParams(dimension_semantics=("arbitrary",) * n_axes,
                                vmem_limit_bytes=VMEM_LIMIT)


def _const_spec(shape):
    nd = len(shape)
    return pl.BlockSpec(shape, lambda *_: (0,) * nd, pipeline_mode=pl.Buffered(1))


def _pick(n, pref):
    t = min(n, pref)
    while n % t:
        t //= 2
    return t


def _rms(x):
    return x * lax.rsqrt(jnp.mean(x * x, axis=-1, keepdims=True) + NORM_EPS)


def _t5_bucket(rel):
    half = N_BUCKETS // 2
    max_exact = half // 2
    ret = jnp.where(rel > 0, half, 0)
    n = jnp.abs(rel)
    nf = jnp.maximum(n, 1).astype(F32)
    large = max_exact + (jnp.log(nf / max_exact) / math.log(MAX_DIST / max_exact)
                         * (half - max_exact)).astype(jnp.int32)
    large = jnp.minimum(large, half - 1)
    return ret + jnp.where(n < max_exact, n, large)


def _bias_lookup(table, rel):
    onehot = (_t5_bucket(rel)[..., None] == jnp.arange(N_BUCKETS)).astype(F32)
    return jnp.einsum("...b,bh->...h", onehot, table, precision=lax.Precision.HIGHEST)


def _inproj_kernel(x_ref, g_ref, w_ref, wvt_ref, *out_refs):
    h = (_rms(x_ref[...]) * g_ref[...]).astype(BF16)
    c0 = 0
    for o_ref, (_, width, _, scale) in zip(out_refs, _IN_COLS):
        acc = jnp.dot(h, w_ref[:, c0:c0 + width], preferred_element_type=F32)
        if scale != 1.0:
            acc = acc * scale
        o_ref[...] = acc.astype(o_ref.dtype)
        c0 += width
    vt = lax.dot_general(wvt_ref[...], h, (((1,), (1,)), ((), ())), preferred_element_type=F32)
    out_refs[-1][...] = vt.astype(out_refs[-1].dtype)


def _inproj(x2d, g, w_cat, w_vct):
    n, d = x2d.shape
    tm = _pick(n, 512)
    out_shape = [jax.ShapeDtypeStruct((n, w), dt) for _, w, dt, _ in _IN_COLS]
    out_specs = [pl.BlockSpec((tm, w), lambda i: (i, 0)) for _, w, _, _ in _IN_COLS]
    out_shape.append(jax.ShapeDtypeStruct((w_vct.shape[0], n), BF16))
    out_specs.append(pl.BlockSpec((w_vct.shape[0], tm), lambda i: (0, i)))
    return pl.pallas_call(
        _inproj_kernel,
        out_shape=out_shape,
        grid=(n // tm,),
        in_specs=[pl.BlockSpec((tm, d), lambda i: (i, 0)), _const_spec(g.shape), _const_spec(w_cat.shape),
                  _const_spec(w_vct.shape)],
        out_specs=out_specs,
        compiler_params=_cparams(1),
        name="inproj",
    )(x2d, g, w_cat, w_vct)


def _diffattn_kernel(lam_ref, subg_ref, cst_ref, band_ref, q_ref, k_ref, v_ref, o_ref, s_ref, *, lam_init):
    i = pl.program_id(2)
    nq = pl.num_programs(2)
    tq = q_ref.shape[0]
    s_len = k_ref.shape[0]
    lp = lam_ref[...]
    lam = (jnp.exp(jnp.sum(lp[0:1] * lp[1:2], axis=-1, keepdims=True))
           - jnp.exp(jnp.sum(lp[2:3] * lp[3:4], axis=-1, keepdims=True))) + lam_init
    q = q_ref[...]
    k = k_ref[...]
    lane = lax.broadcasted_iota(jnp.int32, (1, LANES), 1)
    zero = jnp.zeros_like(q)
    col = lax.broadcasted_iota(jnp.int32, (1, s_len), 1)
    base = jnp.where(col < i * tq, cst_ref[0:1, 0:1], cst_ref[1:2, 0:1])
    for m in range(2):
        qm = jnp.where((lane < DK_A) == (m == 0), q, zero)
        s_ref[m] = lax.dot_general(qm, k, (((1,), (1,)), ((), ())), preferred_element_type=F32) + base
    for d in (-1, 0, 1):
        kb = i + d

        @pl.when((kb >= 0) & (kb < nq))
        def _(d=d, kb=kb):
            off = pl.multiple_of(kb * tq, tq)
            delta = band_ref[d + 1]
            for m in range(2):
                s_ref[m, :, pl.ds(off, tq)] += delta

    probs, den = [], []
    for m in range(2):
        s = s_ref[m]
        p = jnp.exp2(s - jnp.max(s, axis=-1, keepdims=True))
        probs.append(p)
        den.append(jnp.sum(p, axis=-1, keepdims=True))
    w = (probs[0] - probs[1] * (lam * den[0] / den[1])).astype(BF16)
    o = jnp.dot(w, v_ref[...], preferred_element_type=F32) * (1.0 / den[0])
    o_ref[...] = (_rms(o) * subg_ref[...] * (1.0 - lam_init)).astype(o_ref.dtype)


def _diffattn(qa, ka, va, lam_l, subg_l, bias_a, lam_init):
    bsz, s_len, _ = qa.shape
    tq = _pick(s_len, 256)
    nq = s_len // tq
    ii = jnp.arange(tq)
    rel = jnp.stack([d * tq + ii[None, :] - ii[:, None] for d in (-1, 0, 1)])
    bias_a = bias_a * LOG2E
    table = _bias_lookup(bias_a, rel)
    c_left = bias_a[N_BUCKETS // 2 - 1]
    c_right = bias_a[N_BUCKETS - 1]
    side = jnp.stack([c_left, c_right, c_right])
    band = (table - side[:, None, None, :]).transpose(3, 0, 1, 2)
    cst = jnp.broadcast_to(jnp.stack([c_left, c_right], axis=1)[:, :, None], (H_A, 2, LANES))
    cst = jnp.pad(cst, ((0, 0), (0, 6), (0, 0)))
    return pl.pallas_call(
        functools.partial(_diffattn_kernel, lam_init=lam_init),
        out_shape=jax.ShapeDtypeStruct((bsz, s_len, H_A * DV_A), BF16),
        grid=(bsz, H_A, nq),
        in_specs=[
            _const_spec(lam_l.shape),
            _const_spec(subg_l.shape),
            pl.BlockSpec((None, 8, LANES), lambda b, h, i: (h, 0, 0)),
            pl.BlockSpec((None, 3, tq, tq), lambda b, h, i: (h, 0, 0, 0)),
            pl.BlockSpec((None, tq, LANES), lambda b, h, i: (b, i, h)),
            pl.BlockSpec((None, s_len, LANES), lambda b, h, i: (b, 0, h)),
            pl.BlockSpec((None, s_len, LANES), lambda b, h, i: (b, 0, h)),
        ],
        out_specs=pl.BlockSpec((None, tq, LANES), lambda b, h, i: (b, i, h)),
        scratch_shapes=[pltpu.VMEM((2, tq, s_len), F32)],
        compiler_params=_cparams(3),
        name="diffattn",
    )(lam_l, subg_l, cst, band, qa, ka, va)


def _winattn_kernel(sink_ref, band_ref, q_ref, kp_ref, kc_ref, kn_ref, o_ref, *, s_len):
    i = pl.program_id(1)
    tq = q_ref.shape[0]
    lane = lax.broadcasted_iota(jnp.int32, (1, LANES), 1)
    row = lax.broadcasted_iota(jnp.int32, (GQ_B * tq, 1), 0)

    def tile(edge):
        kv = jnp.concatenate([kp_ref[...], kc_ref[...], kn_ref[...]], axis=0)
        kk = kv[:, :LANES]
        vv = kv[:, LANES:]
        outs = []
        for c in range(KV_B):
            qs = jnp.concatenate([q_ref[:, h * LANES:(h + 1) * LANES] for h in range(c * GQ_B, (c + 1) * GQ_B)],
                                 axis=0)
            s = lax.dot_general(qs, kk, (((1,), (1,)), ((), ())), preferred_element_type=F32)
            s = s + band_ref[c]
            if edge:
                kabs = i * tq - BLOCK_B + lax.broadcasted_iota(jnp.int32, (1, tq + 2 * BLOCK_B), 1)
                s = jnp.where((kabs >= 0) & (kabs < s_len), s, -jnp.inf)
            sk = sink_ref[c * GQ_B:c * GQ_B + 1, 0:1]
            for g in range(1, GQ_B):
                sk = jnp.where(row >= g * tq, sink_ref[c * GQ_B + g:c * GQ_B + g + 1, 0:1], sk)
            m = jnp.maximum(jnp.max(s, axis=-1, keepdims=True), sk)
            p = jnp.exp2(s - m)
            inv = 1.0 / (jnp.sum(p, axis=-1, keepdims=True) + jnp.exp2(sk - m))
            outs.append(jnp.dot(p.astype(BF16), vv, preferred_element_type=F32) * inv)
        for g in range(GQ_B):
            o_ref[:, g * LANES:(g + 1) * LANES] = jnp.where(
                lane < DH_B, outs[0][g * tq:(g + 1) * tq], outs[1][g * tq:(g + 1) * tq]).astype(o_ref.dtype)

    is_edge = (i == 0) | (i == pl.num_programs(1) - 1)
    pl.when(is_edge)(lambda: tile(True))
    pl.when(jnp.logical_not(is_edge))(lambda: tile(False))


def _winattn(qb, kvb, sink_l, bias_b):
    bsz, s_len, _ = qb.shape
    nb = s_len // BLOCK_B
    tq = _pick(s_len, 512)
    r = tq // BLOCK_B
    rel = jnp.arange(tq + 2 * BLOCK_B)[None, :] - BLOCK_B - jnp.arange(tq)[:, None]
    band = jnp.where((jnp.abs(rel) <= WINDOW)[:, :, None], _bias_lookup(bias_b * LOG2E, rel), -jnp.inf)
    band = band.transpose(2, 0, 1).reshape(KV_B, GQ_B * tq, tq + 2 * BLOCK_B)
    sink = jnp.broadcast_to(sink_l[:, None] * LOG2E, (H_B, LANES))
    kvw = 2 * KV_B * DH_B
    return pl.pallas_call(
        functools.partial(_winattn_kernel, s_len=s_len),
        out_shape=jax.ShapeDtypeStruct((bsz, s_len, H_B * DH_B), BF16),
        grid=(bsz, s_len // tq),
        in_specs=[
            _const_spec(sink.shape),
            _const_spec(band.shape),
            pl.BlockSpec((None, tq, H_B * LANES), lambda b, i: (b, i, 0)),
            pl.BlockSpec((None, BLOCK_B, kvw), lambda b, i: (b, jnp.maximum(i * r - 1, 0), 0)),
            pl.BlockSpec((None, tq, kvw), lambda b, i: (b, i, 0)),
            pl.BlockSpec((None, BLOCK_B, kvw), lambda b, i: (b, jnp.minimum((i + 1) * r, nb - 1), 0)),
        ],
        out_specs=pl.BlockSpec((None, tq, H_B * DH_B), lambda b, i: (b, i, 0)),
        compiler_params=_cparams(2),
        name="winattn",
    )(sink, band, qb, kvb, kvb, kvb)


def _gla_kernel(q_ref, k_ref, v_ref, vt_ref, a_ref, r_ref, wbd_ref, bup_ref, ng_ref, o_ref,
                g_scr, acc_scr, qd_scr, kv_scr, dec_scr, tot_scr):
    s_len = q_ref.shape[0]
    nc = s_len // CHUNK
    pair = 2 * CHUNK
    pre = jnp.dot(a_ref[...].astype(BF16), wbd_ref[...], preferred_element_type=F32) + bup_ref[...]
    g_scr[...] = (jnp.minimum(pre, 0.0) - jnp.log1p(jnp.exp(-jnp.abs(pre)))) * (1.0 / GATE_TAU)
    lane = lax.broadcasted_iota(jnp.int32, (1, LANES), 1)
    fwd_lane = lane < DK_C
    first = lax.broadcasted_iota(jnp.int32, (pair, 1), 0) < CHUNK
    ri = lax.broadcasted_iota(jnp.int32, (2 * pair, pair), 0)
    ci = lax.broadcasted_iota(jnp.int32, (2 * pair, pair), 1)
    sh = CHUNK.bit_length() - 1
    blk, rr, cc = ri >> sh, ri & (CHUNK - 1), ci & (CHUNK - 1)
    same = (ci >> sh) == (blk & 1)
    cum = jnp.where(same & (((blk < 2) & (cc <= rr)) | ((blk >= 2) & (cc >= rr))), 1.0, 0.0).astype(BF16)
    qrow = lax.broadcasted_iota(jnp.int32, (pair, LANES), 0) & (CHUNK - 1)
    kcol = lax.broadcasted_iota(jnp.int32, (pair, LANES), 1)
    keep = ((kcol < CHUNK) & (kcol <= qrow)) | ((kcol >= CHUNK) & (kcol - CHUNK >= qrow))
    nt = (((1,), (1,)), ((), ()))

    def cumsum(p, carry):
        off = pl.multiple_of(p * pair, pair)
        g = g_scr[pl.ds(off, pair), :]
        g1 = g.astype(BF16)
        r1 = g - g1.astype(F32)
        g2 = r1.astype(BF16)
        g3 = (r1 - g2.astype(F32)).astype(BF16)
        b3 = jnp.dot(cum, jnp.concatenate([g1, g2, g3], axis=1), preferred_element_type=F32)
        bb = b3[:, :LANES] + b3[:, LANES:2 * LANES] + b3[:, 2 * LANES:]
        g_scr[pl.ds(off, pair), :] = jnp.where(fwd_lane, bb[:pair], bb[pair:])
        tot_scr[2 * p] = jnp.where(fwd_lane, bb[CHUNK - 1:CHUNK], bb[pair:pair + 1])
        tot_scr[2 * p + 1] = jnp.where(fwd_lane, bb[pair - 1:pair], bb[pair + CHUNK:pair + CHUNK + 1])
        return carry

    lax.fori_loop(0, nc // 2, cumsum, 0, unroll=4)

    def phase1(p, carry):
        off = pl.multiple_of(p * pair, pair)
        b = g_scr[pl.ds(off, pair), :]
        t0 = tot_scr[2 * p]
        t1 = tot_scr[2 * p + 1]
        q = q_ref[pl.ds(off, pair), :]
        k = k_ref[pl.ds(off, pair), :]
        v = v_ref[pl.ds(off, pair), :]
        qd = q * jnp.exp(b)
        kd = k * jnp.exp(-b)
        ki = (k * jnp.exp(jnp.where(first, t0, t1) - b)).astype(BF16)
        qd_f = jnp.where(fwd_lane, qd, 0.0).astype(BF16)
        qd_b = jnp.where(fwd_lane, 0.0, qd).astype(BF16)
        qd_scr[2 * p] = jnp.concatenate([qd_f[:CHUNK], qd_b[:CHUNK]], axis=0)
        qd_scr[2 * p + 1] = jnp.concatenate([qd_f[CHUNK:], qd_b[CHUNK:]], axis=0)
        kd_f = jnp.where(fwd_lane, kd, 0.0).astype(BF16)
        kd_b = jnp.where(fwd_lane, 0.0, kd).astype(BF16)
        kd4 = jnp.concatenate([kd_f[:CHUNK], kd_b[:CHUNK], kd_f[CHUNK:], kd_b[CHUNK:]], axis=0)
        sc = lax.dot_general(qd.astype(BF16), kd4, nt, preferred_element_type=F32)
        a = jnp.where(keep, jnp.where(first, sc[:, :LANES], sc[:, LANES:]), 0.0).astype(BF16)
        zero = jnp.zeros_like(a)
        a4 = jnp.concatenate([jnp.where(first, a, zero), jnp.where(first, zero, a)], axis=1)
        v4 = jnp.concatenate([v[:CHUNK], v[:CHUNK], v[CHUNK:], v[CHUNK:]], axis=0)
        acc_scr[pl.ds(off, pair), :] = jnp.dot(a4, v4, preferred_element_type=F32)
        vt = vt_ref[:, pl.ds(off, pair)]
        vzero = jnp.zeros_like(vt)
        kv_scr[2 * p] = jnp.dot(jnp.where(fwd_lane, vt, vzero), ki, preferred_element_type=F32)
        kv_scr[2 * p + 1] = jnp.dot(jnp.where(fwd_lane, vzero, vt), ki, preferred_element_type=F32)
        dec_scr[2 * p] = jnp.exp(t0)
        dec_scr[2 * p + 1] = jnp.exp(t1)
        return carry

    lax.fori_loop(0, nc // 2, phase1, 0, unroll=4)

    def phase2(t, state):
        cf = t
        cb = nc - 1 - t
        lhs = jnp.concatenate([qd_scr[cf, 0:CHUNK, :], qd_scr[cb, CHUNK:2 * CHUNK, :]], axis=0)
        o2 = lax.dot_general(lhs, state.astype(BF16), nt, preferred_element_type=F32)
        acc_scr[pl.ds(pl.multiple_of(cf * CHUNK, CHUNK), CHUNK), :] += o2[:CHUNK]
        acc_scr[pl.ds(pl.multiple_of(cb * CHUNK, CHUNK), CHUNK), :] += o2[CHUNK:]
        dec = jnp.where(fwd_lane, dec_scr[cf], dec_scr[cb])
        inc = jnp.where(fwd_lane, kv_scr[cf], kv_scr[cb])
        return dec * state + inc

    lax.fori_loop(0, nc, phase2, jnp.zeros((DV_C, LANES), F32), unroll=8)
    r = r_ref[...].astype(F32)
    o_ref[...] = (_rms(acc_scr[...]) * ng_ref[...] * (r * jax.nn.sigmoid(r))).astype(o_ref.dtype)


def _gla(qc, kc, vc, vct, ac, rc, wbd, bup, ng):
    bsz, s_len, _ = qc.shape
    nc = s_len // CHUNK
    blk = lambda: pl.BlockSpec((None, s_len, LANES), lambda b, h: (b, 0, h))
    per_head = lambda r: pl.BlockSpec((None, r, LANES), lambda b, h: (h, 0, 0))
    return pl.pallas_call(
        _gla_kernel,
        out_shape=jax.ShapeDtypeStruct((bsz, s_len, H_C * DV_C), BF16),
        grid=(bsz, H_C),
        in_specs=[blk(), blk(), blk(),
                  pl.BlockSpec((DV_C, s_len), lambda b, h: (h, b)),
                  pl.BlockSpec((None, s_len, LANES), lambda b, h: (b, 0, 0)),
                  blk(), per_head(LANES), per_head(1), per_head(1)],
        out_specs=blk(),
        scratch_shapes=[pltpu.VMEM((s_len, LANES), F32), pltpu.VMEM((s_len, LANES), F32),
                        pltpu.VMEM((nc, 2 * CHUNK, LANES), BF16), pltpu.VMEM((nc, DV_C, LANES), F32),
                        pltpu.VMEM((nc, 1, LANES), F32), pltpu.VMEM((nc, 1, LANES), F32)],
        compiler_params=_cparams(2),
        name="gla",
    )(qc, kc, vc, vct, ac, rc, wbd, bup, ng)


def _merge_kernel(x_ref, oa_ref, ob_ref, oc_ref, gmix_ref, wmg_ref, wbr_ref, wout_ref, gffn_ref, wr_ref, br_ref,
                  z_ref, aff_ref):
    x = x_ref[...]
    d = x.shape[-1]
    h = (_rms(x) * gmix_ref[...]).astype(BF16)
    acc = None
    for g, o_ref in enumerate((oa_ref, ob_ref, oc_ref)):
        gate = jax.nn.sigmoid(jnp.dot(h, wmg_ref[:, g * D_MODEL:(g + 1) * D_MODEL], preferred_element_type=F32))
        br = jnp.dot(o_ref[...], wbr_ref[g], preferred_element_type=F32)
        acc = gate * br if acc is None else acc + gate * br
    x1 = x + jnp.dot(acc.astype(BF16), wout_ref[...], preferred_element_type=F32)
    xt = _rms(x1) * gffn_ref[...]
    z_ref[:, :d] = x1
    z_ref[:, d:] = xt
    logits = jnp.dot(xt.astype(BF16), wr_ref[...], preferred_element_type=F32) + br_ref[...]
    lane = lax.broadcasted_iota(jnp.int32, logits.shape, 1)
    logits = jnp.where(lane < N_EXPERTS, logits, -jnp.inf)
    e = jnp.exp(logits - jnp.max(logits, axis=-1, keepdims=True))
    aff_ref[...] = (e / jnp.sum(e, axis=-1, keepdims=True)).T


def _merge(x2d, oa, ob, oc, gmix, wmg, wbr, wout, gffn, wr, br):
    n, d = x2d.shape
    tm = _pick(n, 512)
    row = lambda w: pl.BlockSpec((tm, w), lambda i: (i, 0))
    return pl.pallas_call(
        _merge_kernel,
        out_shape=[jax.ShapeDtypeStruct((n, 2 * d), F32), jax.ShapeDtypeStruct((LANES, n), F32)],
        grid=(n // tm,),
        in_specs=[row(d), row(BRANCH_W), row(BRANCH_W), row(BRANCH_W)]
        + [_const_spec(a.shape) for a in (gmix, wmg, wbr, wout, gffn, wr, br)],
        out_specs=[row(2 * d), pl.BlockSpec((LANES, tm), lambda i: (0, i))],
        compiler_params=_cparams(1),
        name="merge",
    )(x2d, oa, ob, oc, gmix, wmg, wbr, wout, gffn, wr, br)


def _prefix_counts(m):
    r = m.shape[0]
    li = lax.broadcasted_iota(jnp.int32, (LANES, LANES), 0)
    lj = lax.broadcasted_iota(jnp.int32, (LANES, LANES), 1)
    within = jnp.dot(m.astype(BF16), (li <= lj).astype(BF16), preferred_element_type=F32)
    rowtot = jnp.broadcast_to(within[:, LANES - 1:LANES], (r, LANES))
    ri = lax.broadcasted_iota(jnp.int32, (r, r), 0)
    rj = lax.broadcasted_iota(jnp.int32, (r, r), 1)
    before = jnp.dot((rj < ri).astype(F32), rowtot, preferred_element_type=F32, precision=lax.Precision.HIGHEST)
    return within, rowtot, before


def _select_kernel(aff_all_ref, aff_ref, idx_ref, gate_ref, thr_scr, *, cap):
    ex = pl.program_id(0)

    @pl.when(ex == 0)
    def _():
        bits_all = pltpu.bitcast(aff_all_ref[...], jnp.int32)

        def bit_step(t, thr):
            cand = thr | jnp.left_shift(jnp.int32(1), 30 - t)
            hit = (bits_all >= cand).astype(jnp.int32)
            cnt = jnp.sum(jnp.sum(hit, axis=1, keepdims=True), axis=2, keepdims=True)
            return jnp.where(cnt >= cap, cand, thr)

        thr_all = lax.fori_loop(0, 31, bit_step, jnp.zeros((aff_all_ref.shape[0], 1, 1), jnp.int32))
        thr_scr[...] = jnp.broadcast_to(thr_all, thr_scr.shape)

    aff = aff_ref[...]
    r = aff.shape[0]
    bits = pltpu.bitcast(aff, jnp.int32)
    thr = thr_scr[ex][:, 0:1]
    gt = bits > thr
    eq = bits == thr
    need = (cap - jnp.sum(gt.astype(jnp.int32), keepdims=True)).astype(F32)
    eqf = eq.astype(F32)
    w_eq, _, b_eq = _prefix_counts(eqf)
    sel = gt | (eq & ((w_eq - eqf + b_eq) < need))
    within, rowtot, before = _prefix_counts(sel.astype(F32))
    row_incl = before[:, 0:1] + rowtot[:, 0:1]
    slot = lax.broadcasted_iota(jnp.int32, (1, cap), 1).astype(F32)
    done = row_incl <= slot
    r_j = jnp.sum(done.astype(F32), axis=0, keepdims=True)
    off_j = jnp.sum(jnp.where(done, rowtot[:, 0:1], 0.0), axis=0, keepdims=True)
    onehot = lax.broadcasted_iota(jnp.int32, (r, cap), 0).astype(F32) == r_j
    cnt_t = jnp.dot(within.T.astype(BF16), onehot.astype(BF16), preferred_element_type=F32)
    lane_j = jnp.sum((cnt_t <= (slot - off_j)).astype(F32), axis=0, keepdims=True)
    idx_ref[...] = (r_j * LANES + lane_j).astype(jnp.int32)
    aff_t = jnp.dot(aff.T, onehot.astype(F32), preferred_element_type=F32, precision=lax.Precision.HIGHEST)
    lsub = lax.broadcasted_iota(jnp.int32, (LANES, cap), 0).astype(F32)
    gate_ref[...] = jnp.sum(jnp.where(lsub == lane_j, aff_t, 0.0), axis=0, keepdims=True)


def _select(aff3, cap):
    e, r, _ = aff3.shape
    return pl.pallas_call(
        functools.partial(_select_kernel, cap=cap),
        out_shape=[jax.ShapeDtypeStruct((e, 1, cap), jnp.int32), jax.ShapeDtypeStruct((e, 1, cap), F32)],
        grid=(e,),
        in_specs=[_const_spec(aff3.shape), pl.BlockSpec((None, r, LANES), lambda i: (i, 0, 0))],
        out_specs=[pl.BlockSpec((None, 1, cap), lambda i: (i, 0, 0)), pl.BlockSpec((None, 1, cap), lambda i: (i, 0, 0))],
        scratch_shapes=[pltpu.VMEM((e, 1, LANES), jnp.int32)],
        compiler_params=_cparams(1),
        name="select",
    )(aff3, aff3)


def _moe_kernel(idx_ref, gate_ref, z_in_hbm, wg_ref, wu_ref, wd_ref, z_hbm, zbuf, obuf, gsem, ssem, *,
                tile, ff_chunk, nt):
    del z_in_hbm
    e = pl.program_id(0)
    s = pl.program_id(1)
    d = obuf.shape[-1]
    slot = s % 2
    n_ff = D_FF // ff_chunk

    groups = tile // ROWS_PER_TILE
    sh = ROWS_PER_TILE.bit_length() - 1

    def issue(step, buf_slot, gather):
        base = (e * nt + step) * tile

        def body(g, carry):
            for k in range(ROWS_PER_TILE):
                t = idx_ref[base + g * ROWS_PER_TILE + k]
                hi, lo = t >> sh, t & (ROWS_PER_TILE - 1)
                if gather:
                    pltpu.make_async_copy(z_hbm.at[hi, pl.ds(lo, 1)], zbuf.at[buf_slot, g, pl.ds(k, 1)],
                                          gsem.at[buf_slot]).start()
                else:
                    pltpu.make_async_copy(obuf.at[buf_slot, g, pl.ds(k, 1)], z_hbm.at[hi, pl.ds(lo, 1), pl.ds(0, d)],
                                          ssem.at[buf_slot]).start()
            return carry

        lax.fori_loop(0, groups, body, 0)

    def wait_gather(src_slot):
        pltpu.make_async_copy(z_hbm.at[pl.ds(0, groups)], zbuf.at[src_slot], gsem.at[src_slot]).wait()

    def wait_scatter(src_slot):
        pltpu.make_async_copy(obuf.at[src_slot], z_hbm.at[pl.ds(0, groups), :, pl.ds(0, d)], ssem.at[src_slot]).wait()

    @pl.when(s == 0)
    def _():
        issue(s, slot, True)

    wait_gather(slot)

    @pl.when(s + 1 < nt)
    def _():
        issue(s + 1, 1 - slot, True)

    @pl.when(s >= 2)
    def _():
        wait_scatter(slot)

    zt = zbuf[slot].reshape(tile, 2 * d)
    x = zt[:, d:].astype(BF16)
    ye = None
    for c0 in range(0, D_FF, ff_chunk):
        hg = jnp.dot(x, wg_ref[:, c0:c0 + ff_chunk], preferred_element_type=F32)
        hu = jnp.dot(x, wu_ref[:, c0:c0 + ff_chunk], preferred_element_type=F32)
        hid = (hg * jax.nn.sigmoid(hg) * hu).astype(BF16)
        part = jnp.dot(hid, wd_ref[c0:c0 + ff_chunk, :], preferred_element_type=F32)
        ye = part if ye is None else ye + part
    obuf[slot] = (zt[:, :d] + ye * gate_ref[...]).reshape(groups, ROWS_PER_TILE, d)
    issue(s, slot, False)

    @pl.when(s == nt - 1)
    def _():
        if nt > 1:
            wait_scatter(1 - slot)
        wait_scatter(slot)


def _moe(idx_flat, gate_col, z, layer, wg, wu, wd, cap):
    n, d2 = z.shape
    d = d2 // 2
    tile = _pick(cap, MOE_TILE)
    grid_spec = pltpu.PrefetchScalarGridSpec(
        num_scalar_prefetch=1,
        grid=(N_EXPERTS, cap // tile),
        in_specs=[
            pl.BlockSpec((None, tile, 1), lambda e, s, idx: (e, s, 0)),
            pl.BlockSpec(memory_space=pl.ANY),
            pl.BlockSpec((None, None, d, D_FF), lambda e, s, idx: (layer, e, 0, 0)),
            pl.BlockSpec((None, None, d, D_FF), lambda e, s, idx: (layer, e, 0, 0)),
            pl.BlockSpec((None, None, D_FF, d), lambda e, s, idx: (layer, e, 0, 0)),
        ],
        out_specs=pl.BlockSpec(memory_space=pl.ANY),
        scratch_shapes=[pltpu.VMEM((2, tile // ROWS_PER_TILE, ROWS_PER_TILE, d2), F32),
                        pltpu.VMEM((2, tile // ROWS_PER_TILE, ROWS_PER_TILE, d), F32),
                        pltpu.SemaphoreType.DMA((2,)), pltpu.SemaphoreType.DMA((2,))],
    )
    z3 = z.reshape(n // ROWS_PER_TILE, ROWS_PER_TILE, d2)
    return pl.pallas_call(
        functools.partial(_moe_kernel, tile=tile, ff_chunk=512, nt=cap // tile),
        out_shape=jax.ShapeDtypeStruct(z3.shape, F32),
        grid_spec=grid_spec,
        input_output_aliases={2: 0},
        compiler_params=_cparams(2),
        name="moe",
    )(idx_flat, gate_col, z3, wg, wu, wd).reshape(n, d2)


def _pe_kernel(x_ref, pe_ref, wpg_ref, wpp_ref, gfin_ref, o_ref, *, last):
    x = x_ref[...]
    gate = jax.nn.sigmoid(jnp.dot(x.astype(BF16), wpg_ref[...], preferred_element_type=F32))
    x = x + gate * jnp.dot(pe_ref[...].astype(BF16), wpp_ref[...], preferred_element_type=F32)
    if last:
        x = _rms(x) * gfin_ref[...]
    o_ref[...] = x


def _pe(z, pe3d, layer, wpg, wpp, gfin, last):
    n, d = z.shape[0], z.shape[1] // 2
    tm = _pick(n, 512)
    return pl.pallas_call(
        functools.partial(_pe_kernel, last=last),
        out_shape=jax.ShapeDtypeStruct((n, d), F32),
        grid=(n // tm,),
        in_specs=[pl.BlockSpec((tm, d), lambda i: (i, 0)), pl.BlockSpec((None, tm, P_DIM), lambda i: (layer, i, 0)),
                  _const_spec(wpg.shape), _const_spec(wpp.shape), _const_spec(gfin.shape)],
        out_specs=pl.BlockSpec((tm, d), lambda i: (i, 0)),
        compiler_params=_cparams(1),
        name="pe",
    )(z, pe3d, wpg, wpp, gfin)


def _pack_weights(l, w_in, w_alpha_up, b_alpha_up, norm_c, w_branch, w_router, b_router):
    widths = (512, 512, 512, 512, 128, 128, 256, 256, 512, 2 * GATE_RANK, 512)
    offs = [0]
    for w in widths:
        offs.append(offs[-1] + w)
    wqa, wka, wva, wqb, wkb, wvb, wqc, wkc, wvc, wac, wrc = (w_in[l][:, offs[i]:offs[i + 1]] for i in range(11))
    zeros64 = jnp.zeros((D_MODEL, DH_B), F32)
    qb_slots = []
    for h in range(H_B):
        wh = wqb[:, h * DH_B:(h + 1) * DH_B]
        qb_slots += [wh, zeros64] if h < GQ_B else [zeros64, wh]
    dup = lambda w: jnp.concatenate([jnp.concatenate([w[:, h * DK_C:(h + 1) * DK_C]] * 2, axis=1) for h in range(H_C)], axis=1)
    w_cat = jnp.concatenate(
        [wqa, wka, wva, jnp.concatenate(qb_slots, axis=1), wkb, wvb, dup(wqc), dup(wkc), wvc,
         jnp.pad(wac, ((0, 0), (0, LANES - 2 * GATE_RANK))), wrc], axis=1).astype(BF16)
    wbd = jnp.zeros((H_C, LANES, LANES), F32)
    bup = []
    for h in range(H_C):
        sl = slice(h * DK_C, (h + 1) * DK_C)
        wbd = wbd.at[h, 0:GATE_RANK, 0:DK_C].set(w_alpha_up[l, 0][:, sl])
        wbd = wbd.at[h, GATE_RANK:2 * GATE_RANK, DK_C:].set(w_alpha_up[l, 1][:, sl])
        bup.append(jnp.concatenate([b_alpha_up[l, 0, sl], b_alpha_up[l, 1, sl]])[None, :])
    bup = jnp.stack(bup)
    ng = norm_c[l][:, None, :]
    order = [h for g in range(GQ_B) for h in (g, GQ_B + g)]
    wb1 = jnp.concatenate([w_branch[l, 1][h * DH_B:(h + 1) * DH_B] for h in order], axis=0)
    wbr = jnp.stack([w_branch[l, 0], wb1, w_branch[l, 2]]).astype(BF16)
    wr = jnp.pad(w_router[l], ((0, 0), (0, LANES - N_EXPERTS))).astype(BF16)
    br = jnp.pad(b_router[l], (0, LANES - N_EXPERTS))[None, :]
    return w_cat, wvc.T.astype(BF16), wbd.astype(BF16), bup, ng, wbr, wr, br


def _trunk(x, pe, rel_bias, g_mix, lam_a, subln_a, sink_b, w_merge_gate, w_out, g_ffn, w_exp_gate, w_exp_up,
           w_exp_down, w_pe_proj, w_pe_gate, g_final, packed):
    bsz, s_len, d = x.shape
    n = bsz * s_len
    cap = EC_FACTOR * n // N_EXPERTS
    bias_a, bias_b = rel_bias[:, :H_A], rel_bias[:, H_A:]
    x2d = x.reshape(n, d)
    for l in range(DEPTH):
        w_cat, w_vct, wbd, bup, ng, wbr, wr, br = packed[l]
        lam_init = 0.8 - 0.6 * math.exp(-0.3 * l)
        *outs, vct = _inproj(x2d, g_mix[l][None, :], w_cat, w_vct)
        qa, ka, va, qb, kvb, qc, kc, vc, ac, rc = (o.reshape(bsz, s_len, o.shape[-1]) for o in outs)
        oa = _diffattn(qa, ka, va, lam_a[l], subln_a[l][None, :], bias_a, lam_init)
        ob = _winattn(qb, kvb, sink_b[l], bias_b)
        oc = _gla(qc, kc, vc, vct, ac, rc, wbd, bup, ng)
        z, aff = _merge(x2d, oa.reshape(n, -1), ob.reshape(n, -1), oc.reshape(n, -1), g_mix[l][None, :],
                        w_merge_gate[l], wbr, w_out[l], g_ffn[l][None, :], wr, br)
        aff3 = aff[:N_EXPERTS].reshape(N_EXPERTS, n // LANES, LANES)
        idx, gates = _select(aff3, cap)
        z = _moe(idx.reshape(-1), gates.reshape(N_EXPERTS, cap, 1), z, l, w_exp_gate, w_exp_up, w_exp_down, cap)
        x2d = _pe(z, pe.reshape(DEPTH, n, -1), l, w_pe_gate[l], w_pe_proj[l], g_final[None, :], l == DEPTH - 1)
    return x2d.reshape(bsz, s_len, d)


def kernel(x_prompt, x_sample, p_prompt, p_sample, rel_bias, g_mix, w_in, lam_a, subln_a, sink_b, w_alpha_up,
           b_alpha_up, norm_c, w_branch, w_merge_gate, w_out, g_ffn, w_router, b_router, w_exp_gate, w_exp_up,
           w_exp_down, w_pe_proj, w_pe_gate, g_final):
    packed = [_pack_weights(l, w_in, w_alpha_up, b_alpha_up, norm_c, w_branch, w_router, b_router)
              for l in range(DEPTH)]
    shared = (rel_bias, g_mix, lam_a, subln_a, sink_b, w_merge_gate.astype(BF16), w_out.astype(BF16), g_ffn,
              w_exp_gate.astype(BF16), w_exp_up.astype(BF16), w_exp_down.astype(BF16), w_pe_proj.astype(BF16),
              w_pe_gate.astype(BF16), g_final, packed)
    y_prompt = _trunk(x_prompt, p_prompt, *shared)
    y_sample = _trunk(x_sample, p_sample, *shared)
    return (y_prompt, y_sample)
```

```python
import functools
import math

import jax
import jax.numpy as jnp
from jax import lax
from jax.experimental import pallas as pl
from jax.experimental.pallas import tpu as pltpu

F32 = jnp.float32
BF16 = jnp.bfloat16

D_MODEL = 1024
DEPTH = 2
H_A, DK_A, DV_A = 4, 64, 128
H_B, KV_B, GQ_B, DH_B = 8, 2, 4, 64
WINDOW, BLOCK_B = 128, 128
H_C, DK_C, DV_C = 4, 64, 128
GATE_RANK, GATE_TAU, CHUNK = 16, 16, 64
BRANCH_W, N_BRANCH = 512, 3
N_BUCKETS, MAX_DIST = 32, 128
N_EXPERTS, EC_FACTOR, D_FF = 16, 2, 2048
P_DIM = 256
NORM_EPS = 1e-6
LOG2E = math.log2(math.e)
LANES = 128
ROWS_PER_TILE = 8
VMEM_LIMIT = 56 * 1024 * 1024
MOE_TILE = 512

_IN_COLS = (
    ("qa", 512, BF16, DK_A ** -0.5 * LOG2E), ("ka", 512, BF16, 1.0), ("va", 512, BF16, 1.0),
    ("qb", 1024, BF16, DH_B ** -0.5 * LOG2E), ("kvb", 256, BF16, 1.0),
    ("qc", 512, F32, DK_C ** -0.5), ("kc", 512, F32, 1.0), ("vc", 512, BF16, 1.0),
    ("ac", 128, F32, 1.0), ("rc", 512, BF16, 1.0),
)


def _cparams(n_axes):
    return pltpu.CompilerParams(dimension_semantics=("arbitrary",) * n_axes,
                                vmem_limit_bytes=VMEM_LIMIT)


def _const_spec(shape):
    nd = len(shape)
    return pl.BlockSpec(shape, lambda *_: (0,) * nd, pipeline_mode=pl.Buffered(1))


def _pick(n, pref):
    t = min(n, pref)
    while n % t:
        t //= 2
    return t


def _rms(x):
    return x * lax.rsqrt(jnp.mean(x * x, axis=-1, keepdims=True) + NORM_EPS)


def _t5_bucket(rel):
    half = N_BUCKETS // 2
    max_exact = half // 2
    ret = jnp.where(rel > 0, half, 0)
    n = jnp.abs(rel)
    nf = jnp.maximum(n, 1).astype(F32)
    large = max_exact + (jnp.log(nf / max_exact) / math.log(MAX_DIST / max_exact)
                         * (half - max_exact)).astype(jnp.int32)
    large = jnp.minimum(large, half - 1)
    return ret + jnp.where(n < max_exact, n, large)


def _bias_lookup(table, rel):
    onehot = (_t5_bucket(rel)[..., None] == jnp.arange(N_BUCKETS)).astype(F32)
    return jnp.einsum("...b,bh->...h", onehot, table, precision=lax.Precision.HIGHEST)


def _inproj_kernel(x_ref, g_ref, w_ref, wvt_ref, *out_refs):
    h = (_rms(x_ref[...]) * g_ref[...]).astype(BF16)
    c0 = 0
    for o_ref, (_, width, _, scale) in zip(out_refs, _IN_COLS):
        acc = jnp.dot(h, w_ref[:, c0:c0 + width], preferred_element_type=F32)
        if scale != 1.0:
            acc = acc * scale
        o_ref[...] = acc.astype(o_ref.dtype)
        c0 += width
    vt = lax.dot_general(wvt_ref[...], h, (((1,), (1,)), ((), ())), preferred_element_type=F32)
    out_refs[-1][...] = vt.astype(out_refs[-1].dtype)


def _inproj(x2d, g, w_cat, w_vct):
    n, d = x2d.shape
    tm = _pick(n, 512)
    out_shape = [jax.ShapeDtypeStruct((n, w), dt) for _, w, dt, _ in _IN_COLS]
    out_specs = [pl.BlockSpec((tm, w), lambda i: (i, 0)) for _, w, _, _ in _IN_COLS]
    out_shape.append(jax.ShapeDtypeStruct((w_vct.shape[0], n), BF16))
    out_specs.append(pl.BlockSpec((w_vct.shape[0], tm), lambda i: (0, i)))
    return pl.pallas_call(
        _inproj_kernel,
        out_shape=out_shape,
        grid=(n // tm,),
        in_specs=[pl.BlockSpec((tm, d), lambda i: (i, 0)), _const_spec(g.shape), _const_spec(w_cat.shape),
                  _const_spec(w_vct.shape)],
        out_specs=out_specs,
        compiler_params=_cparams(1),
        name="inproj",
    )(x2d, g, w_cat, w_vct)


def _diffattn_kernel(lam_ref, subg_ref, cst_ref, band_ref, q_ref, k_ref, v_ref, o_ref, s_ref, *, lam_init):
    i = pl.program_id(2)
    nq = pl.num_programs(2)
    tq = q_ref.shape[0]
    s_len = k_ref.shape[0]
    lp = lam_ref[...]
    lam = (jnp.exp(jnp.sum(lp[0:1] * lp[1:2], axis=-1, keepdims=True))
           - jnp.exp(jnp.sum(lp[2:3] * lp[3:4], axis=-1, keepdims=True))) + lam_init
    q = q_ref[...]
    k = k_ref[...]
    lane = lax.broadcasted_iota(jnp.int32, (1, LANES), 1)
    zero = jnp.zeros_like(q)
    col = lax.broadcasted_iota(jnp.int32, (1, s_len), 1)
    base = jnp.where(col < i * tq, cst_ref[0:1, 0:1], cst_ref[1:2, 0:1])
    for m in range(2):
        qm = jnp.where((lane < DK_A) == (m == 0), q, zero)
        s_ref[m] = lax.dot_general(qm, k, (((1,), (1,)), ((), ())), preferred_element_type=F32) + base
    for d in (-1, 0, 1):
        kb = i + d

        @pl.when((kb >= 0) & (kb < nq))
        def _(d=d, kb=kb):
            off = pl.multiple_of(kb * tq, tq)
            delta = band_ref[d + 1]
            for m in range(2):
                s_ref[m, :, pl.ds(off, tq)] += delta

    rt = tq // 2
    for r0 in range(0, tq, rt):
        probs, den = [], []
        for m in range(2):
            s = s_ref[m, r0:r0 + rt, :]
            p = jnp.exp2(s - jnp.max(s, axis=-1, keepdims=True))
            probs.append(p)
            den.append(jnp.sum(p, axis=-1, keepdims=True))
        w = (probs[0] - probs[1] * (lam * den[0] / den[1])).astype(BF16)
        o = jnp.dot(w, v_ref[...], preferred_element_type=F32) * (1.0 / den[0])
        o_ref[r0:r0 + rt, :] = (_rms(o) * subg_ref[...] * (1.0 - lam_init)).astype(o_ref.dtype)


def _diffattn(qa, ka, va, lam_l, subg_l, bias_a, lam_init):
    bsz, s_len, _ = qa.shape
    tq = _pick(s_len, 256)
    nq = s_len // tq
    ii = jnp.arange(tq)
    rel = jnp.stack([d * tq + ii[None, :] - ii[:, None] for d in (-1, 0, 1)])
    bias_a = bias_a * LOG2E
    table = _bias_lookup(bias_a, rel)
    c_left = bias_a[N_BUCKETS // 2 - 1]
    c_right = bias_a[N_BUCKETS - 1]
    side = jnp.stack([c_left, c_right, c_right])
    band = (table - side[:, None, None, :]).transpose(3, 0, 1, 2)
    cst = jnp.broadcast_to(jnp.stack([c_left, c_right], axis=1)[:, :, None], (H_A, 2, LANES))
    cst = jnp.pad(cst, ((0, 0), (0, 6), (0, 0)))
    return pl.pallas_call(
        functools.partial(_diffattn_kernel, lam_init=lam_init),
        out_shape=jax.ShapeDtypeStruct((bsz, s_len, H_A * DV_A), BF16),
        grid=(bsz, H_A, nq),
        in_specs=[
            _const_spec(lam_l.shape),
            _const_spec(subg_l.shape),
            pl.BlockSpec((None, 8, LANES), lambda b, h, i: (h, 0, 0)),
            pl.BlockSpec((None, 3, tq, tq), lambda b, h, i: (h, 0, 0, 0)),
            pl.BlockSpec((None, tq, LANES), lambda b, h, i: (b, i, h)),
            pl.BlockSpec((None, s_len, LANES), lambda b, h, i: (b, 0, h)),
            pl.BlockSpec((None, s_len, LANES), lambda b, h, i: (b, 0, h)),
        ],
        out_specs=pl.BlockSpec((None, tq, LANES), lambda b, h, i: (b, i, h)),
        scratch_shapes=[pltpu.VMEM((2, tq, s_len), F32)],
        compiler_params=_cparams(3),
        name="diffattn",
    )(lam_l, subg_l, cst, band, qa, ka, va)


def _winattn_kernel(sink_ref, band_ref, q_ref, kp_ref, kc_ref, kn_ref, o_ref, *, s_len):
    i = pl.program_id(1)
    tq = q_ref.shape[0]
    lane = lax.broadcasted_iota(jnp.int32, (1, LANES), 1)
    row = lax.broadcasted_iota(jnp.int32, (GQ_B * tq, 1), 0)

    def tile(edge):
        kv = jnp.concatenate([kp_ref[...], kc_ref[...], kn_ref[...]], axis=0)
        kk = kv[:, :LANES]
        vv = kv[:, LANES:]
        outs = []
        for c in range(KV_B):
            qs = jnp.concatenate([q_ref[:, h * LANES:(h + 1) * LANES] for h in range(c * GQ_B, (c + 1) * GQ_B)],
                                 axis=0)
            s = lax.dot_general(qs, kk, (((1,), (1,)), ((), ())), preferred_element_type=F32)
            s = s + band_ref[c]
            if edge:
                kabs = i * tq - BLOCK_B + lax.broadcasted_iota(jnp.int32, (1, tq + 2 * BLOCK_B), 1)
                s = jnp.where((kabs >= 0) & (kabs < s_len), s, -jnp.inf)
            sk = sink_ref[c * GQ_B:c * GQ_B + 1, 0:1]
            for g in range(1, GQ_B):
                sk = jnp.where(row >= g * tq, sink_ref[c * GQ_B + g:c * GQ_B + g + 1, 0:1], sk)
            m = jnp.maximum(jnp.max(s, axis=-1, keepdims=True), sk)
            p = jnp.exp2(s - m)
            inv = 1.0 / (jnp.sum(p, axis=-1, keepdims=True) + jnp.exp2(sk - m))
            outs.append(jnp.dot(p.astype(BF16), vv, preferred_element_type=F32) * inv)
        for g in range(GQ_B):
            o_ref[:, g * LANES:(g + 1) * LANES] = jnp.where(
                lane < DH_B, outs[0][g * tq:(g + 1) * tq], outs[1][g * tq:(g + 1) * tq]).astype(o_ref.dtype)

    is_edge = (i == 0) | (i == pl.num_programs(1) - 1)
    pl.when(is_edge)(lambda: tile(True))
    pl.when(jnp.logical_not(is_edge))(lambda: tile(False))


def _winattn(qb, kvb, sink_l, bias_b):
    bsz, s_len, _ = qb.shape
    nb = s_len // BLOCK_B
    tq = _pick(s_len, 512)
    r = tq // BLOCK_B
    rel = jnp.arange(tq + 2 * BLOCK_B)[None, :] - BLOCK_B - jnp.arange(tq)[:, None]
    band = jnp.where((jnp.abs(rel) <= WINDOW)[:, :, None], _bias_lookup(bias_b * LOG2E, rel), -jnp.inf)
    band = band.transpose(2, 0, 1).reshape(KV_B, GQ_B * tq, tq + 2 * BLOCK_B)
    sink = jnp.broadcast_to(sink_l[:, None] * LOG2E, (H_B, LANES))
    kvw = 2 * KV_B * DH_B
    return pl.pallas_call(
        functools.partial(_winattn_kernel, s_len=s_len),
        out_shape=jax.ShapeDtypeStruct((bsz, s_len, H_B * DH_B), BF16),
        grid=(bsz, s_len // tq),
        in_specs=[
            _const_spec(sink.shape),
            _const_spec(band.shape),
            pl.BlockSpec((None, tq, H_B * LANES), lambda b, i: (b, i, 0)),
            pl.BlockSpec((None, BLOCK_B, kvw), lambda b, i: (b, jnp.maximum(i * r - 1, 0), 0)),
            pl.BlockSpec((None, tq, kvw), lambda b, i: (b, i, 0)),
            pl.BlockSpec((None, BLOCK_B, kvw), lambda b, i: (b, jnp.minimum((i + 1) * r, nb - 1), 0)),
        ],
        out_specs=pl.BlockSpec((None, tq, H_B * DH_B), lambda b, i: (b, i, 0)),
        compiler_params=_cparams(2),
        name="winattn",
    )(sink, band, qb, kvb, kvb, kvb)


def _gla_kernel(q_ref, k_ref, v_ref, vt_ref, a_ref, r_ref, wbd_ref, bup_ref, ng_ref, o_ref,
                g_scr, acc_scr, qd_scr, kv_scr, dec_scr, tot_scr, a_scr):
    s_len = q_ref.shape[0]
    nc = s_len // CHUNK
    pair = 2 * CHUNK
    pre = jnp.dot(a_ref[...].astype(BF16), wbd_ref[...], preferred_element_type=F32) + bup_ref[...]
    g_scr[...] = (jnp.minimum(pre, 0.0) - jnp.log1p(jnp.exp(-jnp.abs(pre)))) * (1.0 / GATE_TAU)
    lane = lax.broadcasted_iota(jnp.int32, (1, LANES), 1)
    fwd_lane = lane < DK_C
    first = lax.broadcasted_iota(jnp.int32, (pair, 1), 0) < CHUNK
    ri = lax.broadcasted_iota(jnp.int32, (2 * pair, pair), 0)
    ci = lax.broadcasted_iota(jnp.int32, (2 * pair, pair), 1)
    sh = CHUNK.bit_length() - 1
    blk, rr, cc = ri >> sh, ri & (CHUNK - 1), ci & (CHUNK - 1)
    same = (ci >> sh) == (blk & 1)
    cum = jnp.where(same & (((blk < 2) & (cc <= rr)) | ((blk >= 2) & (cc >= rr))), 1.0, 0.0).astype(BF16)
    qrow = lax.broadcasted_iota(jnp.int32, (pair, LANES), 0) & (CHUNK - 1)
    kcol = lax.broadcasted_iota(jnp.int32, (pair, LANES), 1)
    keep = ((kcol < CHUNK) & (kcol <= qrow)) | ((kcol >= CHUNK) & (kcol - CHUNK >= qrow))
    nt = (((1,), (1,)), ((), ()))

    def cumsum(p, carry):
        off = pl.multiple_of(p * pair, pair)
        g = g_scr[pl.ds(off, pair), :]
        g1 = g.astype(BF16)
        r1 = g - g1.astype(F32)
        g2 = r1.astype(BF16)
        g3 = (r1 - g2.astype(F32)).astype(BF16)
        b3 = jnp.dot(cum, jnp.concatenate([g1, g2, g3], axis=1), preferred_element_type=F32)
        bb = b3[:, :LANES] + b3[:, LANES:2 * LANES] + b3[:, 2 * LANES:]
        g_scr[pl.ds(off, pair), :] = jnp.where(fwd_lane, bb[:pair], bb[pair:])
        tot_scr[2 * p] = jnp.where(fwd_lane, bb[CHUNK - 1:CHUNK], bb[pair:pair + 1])
        tot_scr[2 * p + 1] = jnp.where(fwd_lane, bb[pair - 1:pair], bb[pair + CHUNK:pair + CHUNK + 1])
        return carry

    lax.fori_loop(0, nc // 2, cumsum, 0, unroll=4)

    def phase1(p, carry):
        off = pl.multiple_of(p * pair, pair)
        b = g_scr[pl.ds(off, pair), :]
        t0 = tot_scr[2 * p]
        t1 = tot_scr[2 * p + 1]
        q = q_ref[pl.ds(off, pair), :]
        k = k_ref[pl.ds(off, pair), :]
        qd = q * jnp.exp(b)
        kd = k * jnp.exp(-b)
        ki = (k * jnp.exp(jnp.where(first, t0, t1) - b)).astype(BF16)
        qd_f = jnp.where(fwd_lane, qd, 0.0).astype(BF16)
        qd_b = jnp.where(fwd_lane, 0.0, qd).astype(BF16)
        qd_scr[2 * p] = jnp.concatenate([qd_f[:CHUNK], qd_b[:CHUNK]], axis=0)
        qd_scr[2 * p + 1] = jnp.concatenate([qd_f[CHUNK:], qd_b[CHUNK:]], axis=0)
        kd_f = jnp.where(fwd_lane, kd, 0.0).astype(BF16)
        kd_b = jnp.where(fwd_lane, 0.0, kd).astype(BF16)
        kd4 = jnp.concatenate([kd_f[:CHUNK], kd_b[:CHUNK], kd_f[CHUNK:], kd_b[CHUNK:]], axis=0)
        sc = lax.dot_general(qd.astype(BF16), kd4, nt, preferred_element_type=F32)
        a = jnp.where(keep, jnp.where(first, sc[:, :LANES], sc[:, LANES:]), 0.0).astype(BF16)
        zero = jnp.zeros_like(a)
        a_scr[p] = jnp.concatenate([jnp.where(first, a, zero), jnp.where(first, zero, a)], axis=1)
        vt = vt_ref[:, pl.ds(off, pair)]
        vzero = jnp.zeros_like(vt)
        kv_scr[2 * p] = jnp.dot(jnp.where(fwd_lane, vt, vzero), ki, preferred_element_type=F32)
        kv_scr[2 * p + 1] = jnp.dot(jnp.where(fwd_lane, vzero, vt), ki, preferred_element_type=F32)
        dec_scr[2 * p] = jnp.exp(t0)
        dec_scr[2 * p + 1] = jnp.exp(t1)
        return carry

    lax.fori_loop(0, nc // 2, phase1, 0, unroll=4)

    def intra(p, carry):
        off = pl.multiple_of(p * pair, pair)
        v = v_ref[pl.ds(off, pair), :]
        v4 = jnp.concatenate([v[:CHUNK], v[:CHUNK], v[CHUNK:], v[CHUNK:]], axis=0)
        acc_scr[pl.ds(off, pair), :] = jnp.dot(a_scr[p], v4, preferred_element_type=F32)
        return carry

    lax.fori_loop(0, nc // 2, intra, 0, unroll=4)

    def phase2(t, state):
        cf = t
        cb = nc - 1 - t
        lhs = jnp.concatenate([qd_scr[cf, 0:CHUNK, :], qd_scr[cb, CHUNK:2 * CHUNK, :]], axis=0)
        o2 = lax.dot_general(lhs, state.astype(BF16), nt, preferred_element_type=F32)
        acc_scr[pl.ds(pl.multiple_of(cf * CHUNK, CHUNK), CHUNK), :] += o2[:CHUNK]
        acc_scr[pl.ds(pl.multiple_of(cb * CHUNK, CHUNK), CHUNK), :] += o2[CHUNK:]
        dec = jnp.where(fwd_lane, dec_scr[cf], dec_scr[cb])
        inc = jnp.where(fwd_lane, kv_scr[cf], kv_scr[cb])
        return dec * state + inc

    lax.fori_loop(0, nc, phase2, jnp.zeros((DV_C, LANES), F32), unroll=8)
    r = r_ref[...].astype(F32)
    o_ref[...] = (_rms(acc_scr[...]) * ng_ref[...] * (r * jax.nn.sigmoid(r))).astype(o_ref.dtype)


def _gla(qc, kc, vc, vct, ac, rc, wbd, bup, ng):
    bsz, s_len, _ = qc.shape
    nc = s_len // CHUNK
    blk = lambda: pl.BlockSpec((None, s_len, LANES), lambda b, h: (b, 0, h))
    per_head = lambda r: pl.BlockSpec((None, r, LANES), lambda b, h: (h, 0, 0))
    return pl.pallas_call(
        _gla_kernel,
        out_shape=jax.ShapeDtypeStruct((bsz, s_len, H_C * DV_C), BF16),
        grid=(bsz, H_C),
        in_specs=[blk(), blk(), blk(),
                  pl.BlockSpec((DV_C, s_len), lambda b, h: (h, b)),
                  pl.BlockSpec((None, s_len, LANES), lambda b, h: (b, 0, 0)),
                  blk(), per_head(LANES), per_head(1), per_head(1)],
        out_specs=blk(),
        scratch_shapes=[pltpu.VMEM((s_len, LANES), F32), pltpu.VMEM((s_len, LANES), F32),
                        pltpu.VMEM((nc, 2 * CHUNK, LANES), BF16), pltpu.VMEM((nc, DV_C, LANES), F32),
                        pltpu.VMEM((nc, 1, LANES), F32), pltpu.VMEM((nc, 1, LANES), F32),
                        pltpu.VMEM((nc // 2, 2 * CHUNK, 2 * LANES), BF16)],
        compiler_params=_cparams(2),
        name="gla",
    )(qc, kc, vc, vct, ac, rc, wbd, bup, ng)


def _merge_kernel(x_ref, oa_ref, ob_ref, oc_ref, gmix_ref, wmg_ref, wbr_ref, wout_ref, gffn_ref, wr_ref, br_ref,
                  z_ref, aff_ref):
    x = x_ref[...]
    d = x.shape[-1]
    h = (_rms(x) * gmix_ref[...]).astype(BF16)
    acc = None
    for g, o_ref in enumerate((oa_ref, ob_ref, oc_ref)):
        gate = jax.nn.sigmoid(jnp.dot(h, wmg_ref[:, g * D_MODEL:(g + 1) * D_MODEL], preferred_element_type=F32))
        br = jnp.dot(o_ref[...], wbr_ref[g], preferred_element_type=F32)
        acc = gate * br if acc is None else acc + gate * br
    x1 = x + jnp.dot(acc.astype(BF16), wout_ref[...], preferred_element_type=F32)
    xt = _rms(x1) * gffn_ref[...]
    z_ref[:, :d] = x1
    z_ref[:, d:] = xt
    logits = jnp.dot(xt.astype(BF16), wr_ref[...], preferred_element_type=F32) + br_ref[...]
    lane = lax.broadcasted_iota(jnp.int32, logits.shape, 1)
    logits = jnp.where(lane < N_EXPERTS, logits, -jnp.inf)
    e = jnp.exp(logits - jnp.max(logits, axis=-1, keepdims=True))
    aff_ref[...] = (e / jnp.sum(e, axis=-1, keepdims=True)).T


def _merge(x2d, oa, ob, oc, gmix, wmg, wbr, wout, gffn, wr, br):
    n, d = x2d.shape
    tm = _pick(n, 512)
    row = lambda w: pl.BlockSpec((tm, w), lambda i: (i, 0))
    return pl.pallas_call(
        _merge_kernel,
        out_shape=[jax.ShapeDtypeStruct((n, 2 * d), F32), jax.ShapeDtypeStruct((LANES, n), F32)],
        grid=(n // tm,),
        in_specs=[row(d), row(BRANCH_W), row(BRANCH_W), row(BRANCH_W)]
        + [_const_spec(a.shape) for a in (gmix, wmg, wbr, wout, gffn, wr, br)],
        out_specs=[row(2 * d), pl.BlockSpec((LANES, tm), lambda i: (0, i))],
        compiler_params=_cparams(1),
        name="merge",
    )(x2d, oa, ob, oc, gmix, wmg, wbr, wout, gffn, wr, br)


def _prefix_counts(m):
    r = m.shape[0]
    li = lax.broadcasted_iota(jnp.int32, (LANES, LANES), 0)
    lj = lax.broadcasted_iota(jnp.int32, (LANES, LANES), 1)
    within = jnp.dot(m.astype(BF16), (li <= lj).astype(BF16), preferred_element_type=F32)
    rowtot = jnp.broadcast_to(within[:, LANES - 1:LANES], (r, LANES))
    ri = lax.broadcasted_iota(jnp.int32, (r, r), 0)
    rj = lax.broadcasted_iota(jnp.int32, (r, r), 1)
    before = jnp.dot((rj < ri).astype(F32), rowtot, preferred_element_type=F32, precision=lax.Precision.HIGHEST)
    return within, rowtot, before


def _select_kernel(aff_all_ref, aff_ref, idx_ref, gate_ref, thr_scr, *, cap):
    ex = pl.program_id(0)

    @pl.when(ex == 0)
    def _():
        bits_all = pltpu.bitcast(aff_all_ref[...], jnp.int32)

        def bit_step(t, thr):
            cand = thr | jnp.left_shift(jnp.int32(1), 30 - t)
            hit = (bits_all >= cand).astype(jnp.int32)
            cnt = jnp.sum(jnp.sum(hit, axis=1, keepdims=True), axis=2, keepdims=True)
            return jnp.where(cnt >= cap, cand, thr)

        thr_all = lax.fori_loop(0, 31, bit_step, jnp.zeros((aff_all_ref.shape[0], 1, 1), jnp.int32))
        thr_scr[...] = jnp.broadcast_to(thr_all, thr_scr.shape)

    aff = aff_ref[...]
    r = aff.shape[0]
    bits = pltpu.bitcast(aff, jnp.int32)
    thr = thr_scr[ex][:, 0:1]
    gt = bits > thr
    eq = bits == thr
    need = (cap - jnp.sum(gt.astype(jnp.int32), keepdims=True)).astype(F32)
    eqf = eq.astype(F32)
    w_eq, _, b_eq = _prefix_counts(eqf)
    sel = gt | (eq & ((w_eq - eqf + b_eq) < need))
    within, rowtot, before = _prefix_counts(sel.astype(F32))
    row_incl = before[:, 0:1] + rowtot[:, 0:1]
    slot = lax.broadcasted_iota(jnp.int32, (1, cap), 1).astype(F32)
    done = row_incl <= slot
    r_j = jnp.sum(done.astype(F32), axis=0, keepdims=True)
    off_j = jnp.sum(jnp.where(done, rowtot[:, 0:1], 0.0), axis=0, keepdims=True)
    onehot = lax.broadcasted_iota(jnp.int32, (r, cap), 0).astype(F32) == r_j
    cnt_t = jnp.dot(within.T.astype(BF16), onehot.astype(BF16), preferred_element_type=F32)
    lane_j = jnp.sum((cnt_t <= (slot - off_j)).astype(F32), axis=0, keepdims=True)
    idx_ref[...] = (r_j * LANES + lane_j).astype(jnp.int32)
    aff_t = jnp.dot(aff.T, onehot.astype(F32), preferred_element_type=F32, precision=lax.Precision.HIGHEST)
    lsub = lax.broadcasted_iota(jnp.int32, (LANES, cap), 0).astype(F32)
    gate_ref[...] = jnp.sum(jnp.where(lsub == lane_j, aff_t, 0.0), axis=0, keepdims=True)


def _select(aff3, cap):
    e, r, _ = aff3.shape
    return pl.pallas_call(
        functools.partial(_select_kernel, cap=cap),
        out_shape=[jax.ShapeDtypeStruct((e, 1, cap), jnp.int32), jax.ShapeDtypeStruct((e, 1, cap), F32)],
        grid=(e,),
        in_specs=[_const_spec(aff3.shape), pl.BlockSpec((None, r, LANES), lambda i: (i, 0, 0))],
        out_specs=[pl.BlockSpec((None, 1, cap), lambda i: (i, 0, 0)), pl.BlockSpec((None, 1, cap), lambda i: (i, 0, 0))],
        scratch_shapes=[pltpu.VMEM((e, 1, LANES), jnp.int32)],
        compiler_params=_cparams(1),
        name="select",
    )(aff3, aff3)


def _moe_kernel(idx_ref, gate_ref, z_in_hbm, wg_ref, wu_ref, wd_ref, z_hbm, zbuf, obuf, gsem, ssem, *,
                tile, ff_chunk, nt):
    del z_in_hbm
    e = pl.program_id(0)
    s = pl.program_id(1)
    d = obuf.shape[-1]
    slot = s % 2
    n_ff = D_FF // ff_chunk

    groups = tile // ROWS_PER_TILE
    sh = ROWS_PER_TILE.bit_length() - 1

    def issue(step, buf_slot, gather):
        base = (e * nt + step) * tile

        def body(g, carry):
            for k in range(ROWS_PER_TILE):
                t = idx_ref[base + g * ROWS_PER_TILE + k]
                hi, lo = t >> sh, t & (ROWS_PER_TILE - 1)
                if gather:
                    pltpu.make_async_copy(z_hbm.at[hi, pl.ds(lo, 1)], zbuf.at[buf_slot, g, pl.ds(k, 1)],
                                          gsem.at[buf_slot]).start()
                else:
                    pltpu.make_async_copy(obuf.at[buf_slot, g, pl.ds(k, 1)], z_hbm.at[hi, pl.ds(lo, 1), pl.ds(0, d)],
                                          ssem.at[buf_slot]).start()
            return carry

        lax.fori_loop(0, groups, body, 0)

    def wait_gather(src_slot):
        pltpu.make_async_copy(z_hbm.at[pl.ds(0, groups)], zbuf.at[src_slot], gsem.at[src_slot]).wait()

    def wait_scatter(src_slot):
        pltpu.make_async_copy(obuf.at[src_slot], z_hbm.at[pl.ds(0, groups), :, pl.ds(0, d)], ssem.at[src_slot]).wait()

    @pl.when(s == 0)
    def _():
        issue(s, slot, True)

    wait_gather(slot)

    @pl.when(s + 1 < nt)
    def _():
        issue(s + 1, 1 - slot, True)

    @pl.when(s >= 2)
    def _():
        wait_scatter(slot)

    zt = zbuf[slot].reshape(tile, 2 * d)
    x = zt[:, d:].astype(BF16)
    ye = None
    for c0 in range(0, D_FF, ff_chunk):
        hg = jnp.dot(x, wg_ref[:, c0:c0 + ff_chunk], preferred_element_type=F32)
        hu = jnp.dot(x, wu_ref[:, c0:c0 + ff_chunk], preferred_element_type=F32)
        hid = (hg * jax.nn.sigmoid(hg) * hu).astype(BF16)
        part = jnp.dot(hid, wd_ref[c0:c0 + ff_chunk, :], preferred_element_type=F32)
        ye = part if ye is None else ye + part
    obuf[slot] = (zt[:, :d] + ye * gate_ref[...]).reshape(groups, ROWS_PER_TILE, d)
    issue(s, slot, False)

    @pl.when(s == nt - 1)
    def _():
        if nt > 1:
            wait_scatter(1 - slot)
        wait_scatter(slot)


def _moe(idx_flat, gate_col, z, layer, wg, wu, wd, cap):
    n, d2 = z.shape
    d = d2 // 2
    tile = _pick(cap, MOE_TILE)
    grid_spec = pltpu.PrefetchScalarGridSpec(
        num_scalar_prefetch=1,
        grid=(N_EXPERTS, cap // tile),
        in_specs=[
            pl.BlockSpec((None, tile, 1), lambda e, s, idx: (e, s, 0)),
            pl.BlockSpec(memory_space=pl.ANY),
            pl.BlockSpec((None, None, d, D_FF), lambda e, s, idx: (layer, e, 0, 0)),
            pl.BlockSpec((None, None, d, D_FF), lambda e, s, idx: (layer, e, 0, 0)),
            pl.BlockSpec((None, None, D_FF, d), lambda e, s, idx: (layer, e, 0, 0)),
        ],
        out_specs=pl.BlockSpec(memory_space=pl.ANY),
        scratch_shapes=[pltpu.VMEM((2, tile // ROWS_PER_TILE, ROWS_PER_TILE, d2), F32),
                        pltpu.VMEM((2, tile // ROWS_PER_TILE, ROWS_PER_TILE, d), F32),
                        pltpu.SemaphoreType.DMA((2,)), pltpu.SemaphoreType.DMA((2,))],
    )
    z3 = z.reshape(n // ROWS_PER_TILE, ROWS_PER_TILE, d2)
    return pl.pallas_call(
        functools.partial(_moe_kernel, tile=tile, ff_chunk=512, nt=cap // tile),
        out_shape=jax.ShapeDtypeStruct(z3.shape, F32),
        grid_spec=grid_spec,
        input_output_aliases={2: 0},
        compiler_params=_cparams(2),
        name="moe",
    )(idx_flat, gate_col, z3, wg, wu, wd).reshape(n, d2)


def _pe_kernel(x_ref, pe_ref, wpg_ref, wpp_ref, gfin_ref, o_ref, *, last):
    x = x_ref[...]
    gate = jax.nn.sigmoid(jnp.dot(x.astype(BF16), wpg_ref[...], preferred_element_type=F32))
    x = x + gate * jnp.dot(pe_ref[...].astype(BF16), wpp_ref[...], preferred_element_type=F32)
    if last:
        x = _rms(x) * gfin_ref[...]
    o_ref[...] = x


def _pe(z, pe3d, layer, wpg, wpp, gfin, last):
    n, d = z.shape[0], z.shape[1] // 2
    tm = _pick(n, 512)
    return pl.pallas_call(
        functools.partial(_pe_kernel, last=last),
        out_shape=jax.ShapeDtypeStruct((n, d), F32),
        grid=(n // tm,),
        in_specs=[pl.BlockSpec((tm, d), lambda i: (i, 0)), pl.BlockSpec((None, tm, P_DIM), lambda i: (layer, i, 0)),
                  _const_spec(wpg.shape), _const_spec(wpp.shape), _const_spec(gfin.shape)],
        out_specs=pl.BlockSpec((tm, d), lambda i: (i, 0)),
        compiler_params=_cparams(1),
        name="pe",
    )(z, pe3d, wpg, wpp, gfin)


def _pack_weights(l, w_in, w_alpha_up, b_alpha_up, norm_c, w_branch, w_router, b_router):
    widths = (512, 512, 512, 512, 128, 128, 256, 256, 512, 2 * GATE_RANK, 512)
    offs = [0]
    for w in widths:
        offs.append(offs[-1] + w)
    wqa, wka, wva, wqb, wkb, wvb, wqc, wkc, wvc, wac, wrc = (w_in[l][:, offs[i]:offs[i + 1]] for i in range(11))
    zeros64 = jnp.zeros((D_MODEL, DH_B), F32)
    qb_slots = []
    for h in range(H_B):
        wh = wqb[:, h * DH_B:(h + 1) * DH_B]
        qb_slots += [wh, zeros64] if h < GQ_B else [zeros64, wh]
    dup = lambda w: jnp.concatenate([jnp.concatenate([w[:, h * DK_C:(h + 1) * DK_C]] * 2, axis=1) for h in range(H_C)], axis=1)
    w_cat = jnp.concatenate(
        [wqa, wka, wva, jnp.concatenate(qb_slots, axis=1), wkb, wvb, dup(wqc), dup(wkc), wvc,
         jnp.pad(wac, ((0, 0), (0, LANES - 2 * GATE_RANK))), wrc], axis=1).astype(BF16)
    wbd = jnp.zeros((H_C, LANES, LANES), F32)
    bup = []
    for h in range(H_C):
        sl = slice(h * DK_C, (h + 1) * DK_C)
        wbd = wbd.at[h, 0:GATE_RANK, 0:DK_C].set(w_alpha_up[l, 0][:, sl])
        wbd = wbd.at[h, GATE_RANK:2 * GATE_RANK, DK_C:].set(w_alpha_up[l, 1][:, sl])
        bup.append(jnp.concatenate([b_alpha_up[l, 0, sl], b_alpha_up[l, 1, sl]])[None, :])
    bup = jnp.stack(bup)
    ng = norm_c[l][:, None, :]
    order = [h for g in range(GQ_B) for h in (g, GQ_B + g)]
    wb1 = jnp.concatenate([w_branch[l, 1][h * DH_B:(h + 1) * DH_B] for h in order], axis=0)
    wbr = jnp.stack([w_branch[l, 0], wb1, w_branch[l, 2]]).astype(BF16)
    wr = jnp.pad(w_router[l], ((0, 0), (0, LANES - N_EXPERTS))).astype(BF16)
    br = jnp.pad(b_router[l], (0, LANES - N_EXPERTS))[None, :]
    return w_cat, wvc.T.astype(BF16), wbd.astype(BF16), bup, ng, wbr, wr, br


def _trunk(x, pe, rel_bias, g_mix, lam_a, subln_a, sink_b, w_merge_gate, w_out, g_ffn, w_exp_gate, w_exp_up,
           w_exp_down, w_pe_proj, w_pe_gate, g_final, packed):
    bsz, s_len, d = x.shape
    n = bsz * s_len
    cap = EC_FACTOR * n // N_EXPERTS
    bias_a, bias_b = rel_bias[:, :H_A], rel_bias[:, H_A:]
    x2d = x.reshape(n, d)
    for l in range(DEPTH):
        w_cat, w_vct, wbd, bup, ng, wbr, wr, br = packed[l]
        lam_init = 0.8 - 0.6 * math.exp(-0.3 * l)
        *outs, vct = _inproj(x2d, g_mix[l][None, :], w_cat, w_vct)
        qa, ka, va, qb, kvb, qc, kc, vc, ac, rc = (o.reshape(bsz, s_len, o.shape[-1]) for o in outs)
        oa = _diffattn(qa, ka, va, lam_a[l], subln_a[l][None, :], bias_a, lam_init)
        ob = _winattn(qb, kvb, sink_b[l], bias_b)
        oc = _gla(qc, kc, vc, vct, ac, rc, wbd, bup, ng)
        z, aff = _merge(x2d, oa.reshape(n, -1), ob.reshape(n, -1), oc.reshape(n, -1), g_mix[l][None, :],
                        w_merge_gate[l], wbr, w_out[l], g_ffn[l][None, :], wr, br)
        aff3 = aff[:N_EXPERTS].reshape(N_EXPERTS, n // LANES, LANES)
        idx, gates = _select(aff3, cap)
        z = _moe(idx.reshape(-1), gates.reshape(N_EXPERTS, cap, 1), z, l, w_exp_gate, w_exp_up, w_exp_down, cap)
        x2d = _pe(z, pe.reshape(DEPTH, n, -1), l, w_pe_gate[l], w_pe_proj[l], g_final[None, :], l == DEPTH - 1)
    return x2d.reshape(bsz, s_len, d)


def kernel(x_prompt, x_sample, p_prompt, p_sample, rel_bias, g_mix, w_in, lam_a, subln_a, sink_b, w_alpha_up,
           b_alpha_up, norm_c, w_branch, w_merge_gate, w_out, g_ffn, w_router, b_router, w_exp_gate, w_exp_up,
           w_exp_down, w_pe_proj, w_pe_gate, g_final):
    packed = [_pack_weights(l, w_in, w_alpha_up, b_alpha_up, norm_c, w_branch, w_router, b_router)
              for l in range(DEPTH)]
    shared = (rel_bias, g_mix, lam_a, subln_a, sink_b, w_merge_gate.astype(BF16), w_out.astype(BF16), g_ffn,
              w_exp_gate.astype(BF16), w_exp_up.astype(BF16), w_exp_down.astype(BF16), w_pe_proj.astype(BF16),
              w_pe_gate.astype(BF16), g_final, packed)
    y_prompt = _trunk(x_prompt, p_prompt, *shared)
    y_sample = _trunk(x_sample, p_sample, *shared)
    return (y_prompt, y_sample)
```

```python
import functools
import math

import jax
import jax.numpy as jnp
from jax import lax
from jax.experimental import pallas as pl
from jax.experimental.pallas import tpu as pltpu

F32 = jnp.float32
BF16 = jnp.bfloat16

D_MODEL = 1024
DEPTH = 2
H_A, DK_A, DV_A = 4, 64, 128
H_B, KV_B, GQ_B, DH_B = 8, 2, 4, 64
WINDOW, BLOCK_B = 128, 128
H_C, DK_C, DV_C = 4, 64, 128
GATE_RANK, GATE_TAU, CHUNK = 16, 16, 64
BRANCH_W, N_BRANCH = 512, 3
N_BUCKETS, MAX_DIST = 32, 128
N_EXPERTS, EC_FACTOR, D_FF = 16, 2, 2048
P_DIM = 256
NORM_EPS = 1e-6
LOG2E = math.log2(math.e)
LANES = 128
ROWS_PER_TILE = 8
VMEM_LIMIT = 56 * 1024 * 1024
MOE_TILE = 512

_IN_COLS = (
    ("qa", 512, BF16, DK_A ** -0.5 * LOG2E), ("ka", 512, BF16, 1.0), ("va", 512, BF16, 1.0),
    ("qb", 1024, BF16, DH_B ** -0.5 * LOG2E), ("kvb", 256, BF16, 1.0),
    ("qc", 512, F32, DK_C ** -0.5), ("kc", 512, F32, 1.0), ("vc", 512, BF16, 1.0),
    ("ac", 128, F32, 1.0), ("rc", 512, BF16, 1.0),
)


def _cparams(n_axes):
    return pltpu.CompilerParams(dimension_semantics=("arbitrary",) * n_axes,
                                vmem_limit_bytes=VMEM_LIMIT)


def _const_spec(shape):
    nd = len(shape)
    return pl.BlockSpec(shape, lambda *_: (0,) * nd, pipeline_mode=pl.Buffered(1))


def _pick(n, pref):
    t = min(n, pref)
    while n % t:
        t //= 2
    return t


def _rms(x):
    return x * lax.rsqrt(jnp.mean(x * x, axis=-1, keepdims=True) + NORM_EPS)


def _t5_bucket(rel):
    half = N_BUCKETS // 2
    max_exact = half // 2
    ret = jnp.where(rel > 0, half, 0)
    n = jnp.abs(rel)
    nf = jnp.maximum(n, 1).astype(F32)
    large = max_exact + (jnp.log(nf / max_exact) / math.log(MAX_DIST / max_exact)
                         * (half - max_exact)).astype(jnp.int32)
    large = jnp.minimum(large, half - 1)
    return ret + jnp.where(n < max_exact, n, large)


def _bias_lookup(table, rel):
    onehot = (_t5_bucket(rel)[..., None] == jnp.arange(N_BUCKETS)).astype(F32)
    return jnp.einsum("...b,bh->...h", onehot, table, precision=lax.Precision.HIGHEST)


def _inproj_kernel(x_ref, g_ref, w_ref, wvt_ref, *out_refs):
    h = (_rms(x_ref[...]) * g_ref[...]).astype(BF16)
    c0 = 0
    for o_ref, (_, width, _, scale) in zip(out_refs, _IN_COLS):
        acc = jnp.dot(h, w_ref[:, c0:c0 + width], preferred_element_type=F32)
        if scale != 1.0:
            acc = acc * scale
        o_ref[...] = acc.astype(o_ref.dtype)
        c0 += width
    vt = lax.dot_general(wvt_ref[...], h, (((1,), (1,)), ((), ())), preferred_element_type=F32)
    out_refs[-1][...] = vt.astype(out_refs[-1].dtype)


def _inproj(x2d, g, w_cat, w_vct):
    n, d = x2d.shape
    tm = _pick(n, 512)
    out_shape = [jax.ShapeDtypeStruct((n, w), dt) for _, w, dt, _ in _IN_COLS]
    out_specs = [pl.BlockSpec((tm, w), lambda i: (i, 0)) for _, w, _, _ in _IN_COLS]
    out_shape.append(jax.ShapeDtypeStruct((w_vct.shape[0], n), BF16))
    out_specs.append(pl.BlockSpec((w_vct.shape[0], tm), lambda i: (0, i)))
    return pl.pallas_call(
        _inproj_kernel,
        out_shape=out_shape,
        grid=(n // tm,),
        in_specs=[pl.BlockSpec((tm, d), lambda i: (i, 0)), _const_spec(g.shape), _const_spec(w_cat.shape),
                  _const_spec(w_vct.shape)],
        out_specs=out_specs,
        compiler_params=_cparams(1),
        name="inproj",
    )(x2d, g, w_cat, w_vct)


def _diffattn_kernel(lam_ref, subg_ref, cst_ref, band_ref, q_ref, k_ref, v_ref, o_ref, s_ref, *, lam_init):
    i = pl.program_id(2)
    nq = pl.num_programs(2)
    tq = q_ref.shape[0]
    s_len = k_ref.shape[0]
    lp = lam_ref[...]
    lam = (jnp.exp(jnp.sum(lp[0:1] * lp[1:2], axis=-1, keepdims=True))
           - jnp.exp(jnp.sum(lp[2:3] * lp[3:4], axis=-1, keepdims=True))) + lam_init
    q = q_ref[...]
    k = k_ref[...]
    lane = lax.broadcasted_iota(jnp.int32, (1, LANES), 1)
    zero = jnp.zeros_like(q)
    col = lax.broadcasted_iota(jnp.int32, (1, s_len), 1)
    base = jnp.where(col < i * tq, cst_ref[0:1, 0:1], cst_ref[1:2, 0:1])
    for m in range(2):
        qm = jnp.where((lane < DK_A) == (m == 0), q, zero)
        s_ref[m] = lax.dot_general(qm, k, (((1,), (1,)), ((), ())), preferred_element_type=F32) + base
    for d in (-1, 0, 1):
        kb = i + d

        @pl.when((kb >= 0) & (kb < nq))
        def _(d=d, kb=kb):
            off = pl.multiple_of(kb * tq, tq)
            delta = band_ref[d + 1]
            for m in range(2):
                s_ref[m, :, pl.ds(off, tq)] += delta

    probs, den = [], []
    for m in range(2):
        s = s_ref[m]
        p = jnp.exp2(s - jnp.max(s, axis=-1, keepdims=True))
        probs.append(p)
        den.append(jnp.sum(p, axis=-1, keepdims=True))
    w = (probs[0] - probs[1] * (lam * den[0] / den[1])).astype(BF16)
    o = jnp.dot(w, v_ref[...], preferred_element_type=F32) * (1.0 / den[0])
    o_ref[...] = (_rms(o) * subg_ref[...] * (1.0 - lam_init)).astype(o_ref.dtype)


def _diffattn(qa, ka, va, lam_l, subg_l, bias_a, lam_init):
    bsz, s_len, _ = qa.shape
    tq = _pick(s_len, 256)
    nq = s_len // tq
    ii = jnp.arange(tq)
    rel = jnp.stack([d * tq + ii[None, :] - ii[:, None] for d in (-1, 0, 1)])
    bias_a = bias_a * LOG2E
    table = _bias_lookup(bias_a, rel)
    c_left = bias_a[N_BUCKETS // 2 - 1]
    c_right = bias_a[N_BUCKETS - 1]
    side = jnp.stack([c_left, c_right, c_right])
    band = (table - side[:, None, None, :]).transpose(3, 0, 1, 2)
    cst = jnp.broadcast_to(jnp.stack([c_left, c_right], axis=1)[:, :, None], (H_A, 2, LANES))
    cst = jnp.pad(cst, ((0, 0), (0, 6), (0, 0)))
    return pl.pallas_call(
        functools.partial(_diffattn_kernel, lam_init=lam_init),
        out_shape=jax.ShapeDtypeStruct((bsz, s_len, H_A * DV_A), BF16),
        grid=(bsz, H_A, nq),
        in_specs=[
            _const_spec(lam_l.shape),
            _const_spec(subg_l.shape),
            pl.BlockSpec((None, 8, LANES), lambda b, h, i: (h, 0, 0)),
            pl.BlockSpec((None, 3, tq, tq), lambda b, h, i: (h, 0, 0, 0)),
            pl.BlockSpec((None, tq, LANES), lambda b, h, i: (b, i, h)),
            pl.BlockSpec((None, s_len, LANES), lambda b, h, i: (b, 0, h)),
            pl.BlockSpec((None, s_len, LANES), lambda b, h, i: (b, 0, h)),
        ],
        out_specs=pl.BlockSpec((None, tq, LANES), lambda b, h, i: (b, i, h)),
        scratch_shapes=[pltpu.VMEM((2, tq, s_len), F32)],
        compiler_params=_cparams(3),
        name="diffattn",
    )(lam_l, subg_l, cst, band, qa, ka, va)


def _winattn_kernel(sink_ref, band_ref, q_ref, kp_ref, kc_ref, kn_ref, o_ref, *, s_len):
    i = pl.program_id(1)
    tq = q_ref.shape[0]
    lane = lax.broadcasted_iota(jnp.int32, (1, LANES), 1)
    row = lax.broadcasted_iota(jnp.int32, (GQ_B * tq, 1), 0)

    def tile(edge):
        kv = jnp.concatenate([kp_ref[...], kc_ref[...], kn_ref[...]], axis=0)
        kk = kv[:, :LANES]
        vv = kv[:, LANES:]
        outs = []
        for c in range(KV_B):
            qs = jnp.concatenate([q_ref[:, h * LANES:(h + 1) * LANES] for h in range(c * GQ_B, (c + 1) * GQ_B)],
                                 axis=0)
            s = lax.dot_general(qs, kk, (((1,), (1,)), ((), ())), preferred_element_type=F32)
            s = s + band_ref[c]
            if edge:
                kabs = i * tq - BLOCK_B + lax.broadcasted_iota(jnp.int32, (1, tq + 2 * BLOCK_B), 1)
                s = jnp.where((kabs >= 0) & (kabs < s_len), s, -jnp.inf)
            sk = sink_ref[c * GQ_B:c * GQ_B + 1, 0:1]
            for g in range(1, GQ_B):
                sk = jnp.where(row >= g * tq, sink_ref[c * GQ_B + g:c * GQ_B + g + 1, 0:1], sk)
            m = jnp.maximum(jnp.max(s, axis=-1, keepdims=True), sk)
            p = jnp.exp2(s - m)
            inv = 1.0 / (jnp.sum(p, axis=-1, keepdims=True) + jnp.exp2(sk - m))
            outs.append(jnp.dot(p.astype(BF16), vv, preferred_element_type=F32) * inv)
        for g in range(GQ_B):
            o_ref[:, g * LANES:(g + 1) * LANES] = jnp.where(
                lane < DH_B, outs[0][g * tq:(g + 1) * tq], outs[1][g * tq:(g + 1) * tq]).astype(o_ref.dtype)

    is_edge = (i == 0) | (i == pl.num_programs(1) - 1)
    pl.when(is_edge)(lambda: tile(True))
    pl.when(jnp.logical_not(is_edge))(lambda: tile(False))


def _winattn(qb, kvb, sink_l, bias_b):
    bsz, s_len, _ = qb.shape
    nb = s_len // BLOCK_B
    tq = _pick(s_len, 512)
    r = tq // BLOCK_B
    rel = jnp.arange(tq + 2 * BLOCK_B)[None, :] - BLOCK_B - jnp.arange(tq)[:, None]
    band = jnp.where((jnp.abs(rel) <= WINDOW)[:, :, None], _bias_lookup(bias_b * LOG2E, rel), -jnp.inf)
    band = band.transpose(2, 0, 1).reshape(KV_B, GQ_B * tq, tq + 2 * BLOCK_B)
    sink = jnp.broadcast_to(sink_l[:, None] * LOG2E, (H_B, LANES))
    kvw = 2 * KV_B * DH_B
    return pl.pallas_call(
        functools.partial(_winattn_kernel, s_len=s_len),
        out_shape=jax.ShapeDtypeStruct((bsz, s_len, H_B * DH_B), BF16),
        grid=(bsz, s_len // tq),
        in_specs=[
            _const_spec(sink.shape),
            _const_spec(band.shape),
            pl.BlockSpec((None, tq, H_B * LANES), lambda b, i: (b, i, 0)),
            pl.BlockSpec((None, BLOCK_B, kvw), lambda b, i: (b, jnp.maximum(i * r - 1, 0), 0)),
            pl.BlockSpec((None, tq, kvw), lambda b, i: (b, i, 0)),
            pl.BlockSpec((None, BLOCK_B, kvw), lambda b, i: (b, jnp.minimum((i + 1) * r, nb - 1), 0)),
        ],
        out_specs=pl.BlockSpec((None, tq, H_B * DH_B), lambda b, i: (b, i, 0)),
        compiler_params=_cparams(2),
        name="winattn",
    )(sink, band, qb, kvb, kvb, kvb)


def _gla_kernel(q_ref, k_ref, v_ref, vt_ref, a_ref, r_ref, wbd_ref, bup_ref, ng_ref, o_ref,
                g_scr, acc_scr, qd_scr, kv_scr, dec_scr, tot_scr, a_scr):
    s_len = q_ref.shape[0]
    nc = s_len // CHUNK
    pair = 2 * CHUNK
    pre = jnp.dot(a_ref[...].astype(BF16), wbd_ref[...], preferred_element_type=F32) + bup_ref[...]
    g_scr[...] = (jnp.minimum(pre, 0.0) - jnp.log1p(jnp.exp(-jnp.abs(pre)))) * (1.0 / GATE_TAU)
    lane = lax.broadcasted_iota(jnp.int32, (1, LANES), 1)
    fwd_lane = lane < DK_C
    first = lax.broadcasted_iota(jnp.int32, (pair, 1), 0) < CHUNK
    ri = lax.broadcasted_iota(jnp.int32, (2 * pair, pair), 0)
    ci = lax.broadcasted_iota(jnp.int32, (2 * pair, pair), 1)
    sh = CHUNK.bit_length() - 1
    blk, rr, cc = ri >> sh, ri & (CHUNK - 1), ci & (CHUNK - 1)
    same = (ci >> sh) == (blk & 1)
    cum = jnp.where(same & (((blk < 2) & (cc <= rr)) | ((blk >= 2) & (cc >= rr))), 1.0, 0.0).astype(BF16)
    qrow = lax.broadcasted_iota(jnp.int32, (pair, LANES), 0) & (CHUNK - 1)
    kcol = lax.broadcasted_iota(jnp.int32, (pair, LANES), 1)
    keep = ((kcol < CHUNK) & (kcol <= qrow)) | ((kcol >= CHUNK) & (kcol - CHUNK >= qrow))
    nt = (((1,), (1,)), ((), ()))

    def cumsum(p, carry):
        off = pl.multiple_of(p * pair, pair)
        g = g_scr[pl.ds(off, pair), :]
        g1 = g.astype(BF16)
        r1 = g - g1.astype(F32)
        g2 = r1.astype(BF16)
        g3 = (r1 - g2.astype(F32)).astype(BF16)
        b3 = jnp.dot(cum, jnp.concatenate([g1, g2, g3], axis=1), preferred_element_type=F32)
        bb = b3[:, :LANES] + b3[:, LANES:2 * LANES] + b3[:, 2 * LANES:]
        g_scr[pl.ds(off, pair), :] = jnp.where(fwd_lane, bb[:pair], bb[pair:])
        tot_scr[2 * p] = jnp.where(fwd_lane, bb[CHUNK - 1:CHUNK], bb[pair:pair + 1])
        tot_scr[2 * p + 1] = jnp.where(fwd_lane, bb[pair - 1:pair], bb[pair + CHUNK:pair + CHUNK + 1])
        return carry

    lax.fori_loop(0, nc // 2, cumsum, 0, unroll=4)

    def phase1(p, carry):
        off = pl.multiple_of(p * pair, pair)
        b = g_scr[pl.ds(off, pair), :]
        t0 = tot_scr[2 * p]
        t1 = tot_scr[2 * p + 1]
        q = q_ref[pl.ds(off, pair), :]
        k = k_ref[pl.ds(off, pair), :]
        qd = q * jnp.exp(b)
        kd = k * jnp.exp(-b)
        ki = (k * jnp.exp(jnp.where(first, t0, t1) - b)).astype(BF16)
        qd_f = jnp.where(fwd_lane, qd, 0.0).astype(BF16)
        qd_b = jnp.where(fwd_lane, 0.0, qd).astype(BF16)
        qd_scr[2 * p] = jnp.concatenate([qd_f[:CHUNK], qd_b[:CHUNK]], axis=0)
        qd_scr[2 * p + 1] = jnp.concatenate([qd_f[CHUNK:], qd_b[CHUNK:]], axis=0)
        kd_f = jnp.where(fwd_lane, kd, 0.0).astype(BF16)
        kd_b = jnp.where(fwd_lane, 0.0, kd).astype(BF16)
        kd4 = jnp.concatenate([kd_f[:CHUNK], kd_b[:CHUNK], kd_f[CHUNK:], kd_b[CHUNK:]], axis=0)
        sc = lax.dot_general(qd.astype(BF16), kd4, nt, preferred_element_type=F32)
        a = jnp.where(keep, jnp.where(first, sc[:, :LANES], sc[:, LANES:]), 0.0).astype(BF16)
        zero = jnp.zeros_like(a)
        a_scr[p] = jnp.concatenate([jnp.where(first, a, zero), jnp.where(first, zero, a)], axis=1)
        vt = vt_ref[:, pl.ds(off, pair)]
        vzero = jnp.zeros_like(vt)
        kv_scr[2 * p] = jnp.dot(jnp.where(fwd_lane, vt, vzero), ki, preferred_element_type=F32)
        kv_scr[2 * p + 1] = jnp.dot(jnp.where(fwd_lane, vzero, vt), ki, preferred_element_type=F32)
        dec_scr[2 * p] = jnp.exp(t0)
        dec_scr[2 * p + 1] = jnp.exp(t1)
        return carry

    lax.fori_loop(0, nc // 2, phase1, 0, unroll=4)

    def intra(p, carry):
        off = pl.multiple_of(p * pair, pair)
        v = v_ref[pl.ds(off, pair), :]
        v4 = jnp.concatenate([v[:CHUNK], v[:CHUNK], v[CHUNK:], v[CHUNK:]], axis=0)
        acc_scr[pl.ds(off, pair), :] = jnp.dot(a_scr[p], v4, preferred_element_type=F32)
        return carry

    lax.fori_loop(0, nc // 2, intra, 0, unroll=4)

    def phase2(t, state):
        cf = t
        cb = nc - 1 - t
        lhs = jnp.concatenate([qd_scr[cf, 0:CHUNK, :], qd_scr[cb, CHUNK:2 * CHUNK, :]], axis=0)
        o2 = lax.dot_general(lhs, state.astype(BF16), nt, preferred_element_type=F32)
        acc_scr[pl.ds(pl.multiple_of(cf * CHUNK, CHUNK), CHUNK), :] += o2[:CHUNK]
        acc_scr[pl.ds(pl.multiple_of(cb * CHUNK, CHUNK), CHUNK), :] += o2[CHUNK:]
        dec = jnp.where(fwd_lane, dec_scr[cf], dec_scr[cb])
        inc = jnp.where(fwd_lane, kv_scr[cf], kv_scr[cb])
        return dec * state + inc

    lax.fori_loop(0, nc, phase2, jnp.zeros((DV_C, LANES), F32), unroll=8)
    r = r_ref[...].astype(F32)
    o_ref[...] = (_rms(acc_scr[...]) * ng_ref[...] * (r * jax.nn.sigmoid(r))).astype(o_ref.dtype)


def _gla(qc, kc, vc, vct, ac, rc, wbd, bup, ng):
    bsz, s_len, _ = qc.shape
    nc = s_len // CHUNK
    blk = lambda: pl.BlockSpec((None, s_len, LANES), lambda b, h: (b, 0, h))
    per_head = lambda r: pl.BlockSpec((None, r, LANES), lambda b, h: (h, 0, 0))
    return pl.pallas_call(
        _gla_kernel,
        out_shape=jax.ShapeDtypeStruct((bsz, s_len, H_C * DV_C), BF16),
        grid=(bsz, H_C),
        in_specs=[blk(), blk(), blk(),
                  pl.BlockSpec((DV_C, s_len), lambda b, h: (h, b)),
                  pl.BlockSpec((None, s_len, LANES), lambda b, h: (b, 0, 0)),
                  blk(), per_head(LANES), per_head(1), per_head(1)],
        out_specs=blk(),
        scratch_shapes=[pltpu.VMEM((s_len, LANES), F32), pltpu.VMEM((s_len, LANES), F32),
                        pltpu.VMEM((nc, 2 * CHUNK, LANES), BF16), pltpu.VMEM((nc, DV_C, LANES), F32),
                        pltpu.VMEM((nc, 1, LANES), F32), pltpu.VMEM((nc, 1, LANES), F32),
                        pltpu.VMEM((nc // 2, 2 * CHUNK, 2 * LANES), BF16)],
        compiler_params=_cparams(2),
        name="gla",
    )(qc, kc, vc, vct, ac, rc, wbd, bup, ng)


def _merge_kernel(x_ref, oa_ref, ob_ref, oc_ref, gmix_ref, wmg_ref, wbr_ref, wout_ref, gffn_ref, wr_ref, br_ref,
                  z_ref, aff_ref):
    x = x_ref[...]
    d = x.shape[-1]
    h = (_rms(x) * gmix_ref[...]).astype(BF16)
    acc = None
    for g, o_ref in enumerate((oa_ref, ob_ref, oc_ref)):
        gate = jax.nn.sigmoid(jnp.dot(h, wmg_ref[:, g * D_MODEL:(g + 1) * D_MODEL], preferred_element_type=F32))
        br = jnp.dot(o_ref[...], wbr_ref[g], preferred_element_type=F32)
        acc = gate * br if acc is None else acc + gate * br
    x1 = x + jnp.dot(acc.astype(BF16), wout_ref[...], preferred_element_type=F32)
    xt = _rms(x1) * gffn_ref[...]
    z_ref[:, :d] = x1
    z_ref[:, d:] = xt
    logits = jnp.dot(xt.astype(BF16), wr_ref[...], preferred_element_type=F32) + br_ref[...]
    lane = lax.broadcasted_iota(jnp.int32, logits.shape, 1)
    logits = jnp.where(lane < N_EXPERTS, logits, -jnp.inf)
    e = jnp.exp(logits - jnp.max(logits, axis=-1, keepdims=True))
    aff_ref[...] = (e / jnp.sum(e, axis=-1, keepdims=True)).T


def _merge(x2d, oa, ob, oc, gmix, wmg, wbr, wout, gffn, wr, br):
    n, d = x2d.shape
    tm = _pick(n, 512)
    row = lambda w: pl.BlockSpec((tm, w), lambda i: (i, 0))
    return pl.pallas_call(
        _merge_kernel,
        out_shape=[jax.ShapeDtypeStruct((n, 2 * d), F32), jax.ShapeDtypeStruct((LANES, n), F32)],
        grid=(n // tm,),
        in_specs=[row(d), row(BRANCH_W), row(BRANCH_W), row(BRANCH_W)]
        + [_const_spec(a.shape) for a in (gmix, wmg, wbr, wout, gffn, wr, br)],
        out_specs=[row(2 * d), pl.BlockSpec((LANES, tm), lambda i: (0, i))],
        compiler_params=_cparams(1),
        name="merge",
    )(x2d, oa, ob, oc, gmix, wmg, wbr, wout, gffn, wr, br)


def _prefix_counts(m):
    r = m.shape[0]
    li = lax.broadcasted_iota(jnp.int32, (LANES, LANES), 0)
    lj = lax.broadcasted_iota(jnp.int32, (LANES, LANES), 1)
    within = jnp.dot(m.astype(BF16), (li <= lj).astype(BF16), preferred_element_type=F32)
    rowtot = jnp.broadcast_to(within[:, LANES - 1:LANES], (r, LANES))
    ri = lax.broadcasted_iota(jnp.int32, (r, r), 0)
    rj = lax.broadcasted_iota(jnp.int32, (r, r), 1)
    before = jnp.dot((rj < ri).astype(F32), rowtot, preferred_element_type=F32, precision=lax.Precision.HIGHEST)
    return within, rowtot, before


def _select_kernel(aff_all_ref, aff_ref, idx_ref, gate_ref, thr_scr, *, cap):
    ex = pl.program_id(0)

    @pl.when(ex == 0)
    def _():
        bits_all = pltpu.bitcast(aff_all_ref[...], jnp.int32)

        def bit_step(t, thr):
            cand = thr | jnp.left_shift(jnp.int32(1), 30 - t)
            hit = (bits_all >= cand).astype(jnp.int32)
            cnt = jnp.sum(jnp.sum(hit, axis=1, keepdims=True), axis=2, keepdims=True)
            return jnp.where(cnt >= cap, cand, thr)

        thr_all = lax.fori_loop(0, 31, bit_step, jnp.zeros((aff_all_ref.shape[0], 1, 1), jnp.int32))
        thr_scr[...] = jnp.broadcast_to(thr_all, thr_scr.shape)

    aff = aff_ref[...]
    r = aff.shape[0]
    bits = pltpu.bitcast(aff, jnp.int32)
    thr = thr_scr[ex][:, 0:1]
    gt = bits > thr
    eq = bits == thr
    need = (cap - jnp.sum(gt.astype(jnp.int32), keepdims=True)).astype(F32)
    eqf = eq.astype(F32)
    w_eq, _, b_eq = _prefix_counts(eqf)
    sel = gt | (eq & ((w_eq - eqf + b_eq) < need))
    within, rowtot, before = _prefix_counts(sel.astype(F32))
    row_incl = before[:, 0:1] + rowtot[:, 0:1]
    slot = lax.broadcasted_iota(jnp.int32, (1, cap), 1).astype(F32)
    done = row_incl <= slot
    r_j = jnp.sum(done.astype(F32), axis=0, keepdims=True)
    off_j = jnp.sum(jnp.where(done, rowtot[:, 0:1], 0.0), axis=0, keepdims=True)
    onehot = lax.broadcasted_iota(jnp.int32, (r, cap), 0).astype(F32) == r_j
    cnt_t = jnp.dot(within.T.astype(BF16), onehot.astype(BF16), preferred_element_type=F32)
    lane_j = jnp.sum((cnt_t <= (slot - off_j)).astype(F32), axis=0, keepdims=True)
    idx_ref[...] = (r_j * LANES + lane_j).astype(jnp.int32)
    aff_t = jnp.dot(aff.T, onehot.astype(F32), preferred_element_type=F32, precision=lax.Precision.HIGHEST)
    lsub = lax.broadcasted_iota(jnp.int32, (LANES, cap), 0).astype(F32)
    gate_ref[...] = jnp.sum(jnp.where(lsub == lane_j, aff_t, 0.0), axis=0, keepdims=True)


def _select(aff3, cap):
    e, r, _ = aff3.shape
    return pl.pallas_call(
        functools.partial(_select_kernel, cap=cap),
        out_shape=[jax.ShapeDtypeStruct((e, 1, cap), jnp.int32), jax.ShapeDtypeStruct((e, 1, cap), F32)],
        grid=(e,),
        in_specs=[_const_spec(aff3.shape), pl.BlockSpec((None, r, LANES), lambda i: (i, 0, 0))],
        out_specs=[pl.BlockSpec((None, 1, cap), lambda i: (i, 0, 0)), pl.BlockSpec((None, 1, cap), lambda i: (i, 0, 0))],
        scratch_shapes=[pltpu.VMEM((e, 1, LANES), jnp.int32)],
        compiler_params=_cparams(1),
        name="select",
    )(aff3, aff3)


def _moe_kernel(idx_ref, gate_ref, z_in_hbm, wg_ref, wu_ref, wd_ref, z_hbm, zbuf, obuf, gsem, ssem, *,
                tile, ff_chunk, nt):
    del z_in_hbm
    e = pl.program_id(0)
    s = pl.program_id(1)
    d = obuf.shape[-1]
    slot = s % 2
    n_ff = D_FF // ff_chunk

    groups = tile // ROWS_PER_TILE
    sh = ROWS_PER_TILE.bit_length() - 1

    def issue(step, buf_slot, gather):
        base = (e * nt + step) * tile

        def body(g, carry):
            for k in range(ROWS_PER_TILE):
                t = idx_ref[base + g * ROWS_PER_TILE + k]
                hi, lo = t >> sh, t & (ROWS_PER_TILE - 1)
                if gather:
                    pltpu.make_async_copy(z_hbm.at[hi, pl.ds(lo, 1)], zbuf.at[buf_slot, g, pl.ds(k, 1)],
                                          gsem.at[buf_slot]).start()
                else:
                    pltpu.make_async_copy(obuf.at[buf_slot, g, pl.ds(k, 1)], z_hbm.at[hi, pl.ds(lo, 1), pl.ds(0, d)],
                                          ssem.at[buf_slot]).start()
            return carry

        lax.fori_loop(0, groups, body, 0)

    def wait_gather(src_slot):
        pltpu.make_async_copy(z_hbm.at[pl.ds(0, groups)], zbuf.at[src_slot], gsem.at[src_slot]).wait()

    def wait_scatter(src_slot):
        pltpu.make_async_copy(obuf.at[src_slot], z_hbm.at[pl.ds(0, groups), :, pl.ds(0, d)], ssem.at[src_slot]).wait()

    @pl.when(s == 0)
    def _():
        issue(s, slot, True)

    wait_gather(slot)

    @pl.when(s + 1 < nt)
    def _():
        issue(s + 1, 1 - slot, True)

    @pl.when(s >= 2)
    def _():
        wait_scatter(slot)

    zt = zbuf[slot].reshape(tile, 2 * d)
    x = zt[:, d:].astype(BF16)
    ye = None
    for c0 in range(0, D_FF, ff_chunk):
        hg = jnp.dot(x, wg_ref[:, c0:c0 + ff_chunk], preferred_element_type=F32)
        hu = jnp.dot(x, wu_ref[:, c0:c0 + ff_chunk], preferred_element_type=F32)
        hid = (hg * jax.nn.sigmoid(hg) * hu).astype(BF16)
        part = jnp.dot(hid, wd_ref[c0:c0 + ff_chunk, :], preferred_element_type=F32)
        ye = part if ye is None else ye + part
    obuf[slot] = (zt[:, :d] + ye * gate_ref[...]).reshape(groups, ROWS_PER_TILE, d)
    issue(s, slot, False)

    @pl.when(s == nt - 1)
    def _():
        if nt > 1:
            wait_scatter(1 - slot)
        wait_scatter(slot)


def _moe(idx_flat, gate_col, z, layer, wg, wu, wd, cap):
    n, d2 = z.shape
    d = d2 // 2
    tile = _pick(cap, MOE_TILE)
    grid_spec = pltpu.PrefetchScalarGridSpec(
        num_scalar_prefetch=1,
        grid=(N_EXPERTS, cap // tile),
        in_specs=[
            pl.BlockSpec((None, tile, 1), lambda e, s, idx: (e, s, 0)),
            pl.BlockSpec(memory_space=pl.ANY),
            pl.BlockSpec((None, None, d, D_FF), lambda e, s, idx: (layer, e, 0, 0)),
            pl.BlockSpec((None, None, d, D_FF), lambda e, s, idx: (layer, e, 0, 0)),
            pl.BlockSpec((None, None, D_FF, d), lambda e, s, idx: (layer, e, 0, 0)),
        ],
        out_specs=pl.BlockSpec(memory_space=pl.ANY),
        scratch_shapes=[pltpu.VMEM((2, tile // ROWS_PER_TILE, ROWS_PER_TILE, d2), F32),
                        pltpu.VMEM((2, tile // ROWS_PER_TILE, ROWS_PER_TILE, d), F32),
                        pltpu.SemaphoreType.DMA((2,)), pltpu.SemaphoreType.DMA((2,))],
    )
    z3 = z.reshape(n // ROWS_PER_TILE, ROWS_PER_TILE, d2)
    return pl.pallas_call(
        functools.partial(_moe_kernel, tile=tile, ff_chunk=512, nt=cap // tile),
        out_shape=jax.ShapeDtypeStruct(z3.shape, F32),
        grid_spec=grid_spec,
        input_output_aliases={2: 0},
        compiler_params=_cparams(2),
        name="moe",
    )(idx_flat, gate_col, z3, wg, wu, wd).reshape(n, d2)


def _pe_kernel(x_ref, pe_ref, wpg_ref, wpp_ref, gfin_ref, o_ref, *, last):
    x = x_ref[...]
    gate = jax.nn.sigmoid(jnp.dot(x.astype(BF16), wpg_ref[...], preferred_element_type=F32))
    x = x + gate * jnp.dot(pe_ref[...].astype(BF16), wpp_ref[...], preferred_element_type=F32)
    if last:
        x = _rms(x) * gfin_ref[...]
    o_ref[...] = x


def _pe(z, pe3d, layer, wpg, wpp, gfin, last):
    n, d = z.shape[0], z.shape[1] // 2
    tm = _pick(n, 512)
    return pl.pallas_call(
        functools.partial(_pe_kernel, last=last),
        out_shape=jax.ShapeDtypeStruct((n, d), F32),
        grid=(n // tm,),
        in_specs=[pl.BlockSpec((tm, d), lambda i: (i, 0)), pl.BlockSpec((None, tm, P_DIM), lambda i: (layer, i, 0)),
                  _const_spec(wpg.shape), _const_spec(wpp.shape), _const_spec(gfin.shape)],
        out_specs=pl.BlockSpec((tm, d), lambda i: (i, 0)),
        compiler_params=_cparams(1),
        name="pe",
    )(z, pe3d, wpg, wpp, gfin)


def _pack_weights(l, w_in, w_alpha_up, b_alpha_up, norm_c, w_branch, w_router, b_router):
    widths = (512, 512, 512, 512, 128, 128, 256, 256, 512, 2 * GATE_RANK, 512)
    offs = [0]
    for w in widths:
        offs.append(offs[-1] + w)
    wqa, wka, wva, wqb, wkb, wvb, wqc, wkc, wvc, wac, wrc = (w_in[l][:, offs[i]:offs[i + 1]] for i in range(11))
    zeros64 = jnp.zeros((D_MODEL, DH_B), F32)
    qb_slots = []
    for h in range(H_B):
        wh = wqb[:, h * DH_B:(h + 1) * DH_B]
        qb_slots += [wh, zeros64] if h < GQ_B else [zeros64, wh]
    dup = lambda w: jnp.concatenate([jnp.concatenate([w[:, h * DK_C:(h + 1) * DK_C]] * 2, axis=1) for h in range(H_C)], axis=1)
    w_cat = jnp.concatenate(
        [wqa, wka, wva, jnp.concatenate(qb_slots, axis=1), wkb, wvb, dup(wqc), dup(wkc), wvc,
         jnp.pad(wac, ((0, 0), (0, LANES - 2 * GATE_RANK))), wrc], axis=1).astype(BF16)
    wbd = jnp.zeros((H_C, LANES, LANES), F32)
    bup = []
    for h in range(H_C):
        sl = slice(h * DK_C, (h + 1) * DK_C)
        wbd = wbd.at[h, 0:GATE_RANK, 0:DK_C].set(w_alpha_up[l, 0][:, sl])
        wbd = wbd.at[h, GATE_RANK:2 * GATE_RANK, DK_C:].set(w_alpha_up[l, 1][:, sl])
        bup.append(jnp.concatenate([b_alpha_up[l, 0, sl], b_alpha_up[l, 1, sl]])[None, :])
    bup = jnp.stack(bup)
    ng = norm_c[l][:, None, :]
    order = [h for g in range(GQ_B) for h in (g, GQ_B + g)]
    wb1 = jnp.concatenate([w_branch[l, 1][h * DH_B:(h + 1) * DH_B] for h in order], axis=0)
    wbr = jnp.stack([w_branch[l, 0], wb1, w_branch[l, 2]]).astype(BF16)
    wr = jnp.pad(w_router[l], ((0, 0), (0, LANES - N_EXPERTS))).astype(BF16)
    br = jnp.pad(b_router[l], (0, LANES - N_EXPERTS))[None, :]
    return w_cat, wvc.T.astype(BF16), wbd.astype(BF16), bup, ng, wbr, wr, br


def _trunk(x, pe, rel_bias, g_mix, lam_a, subln_a, sink_b, w_merge_gate, w_out, g_ffn, w_exp_gate, w_exp_up,
           w_exp_down, w_pe_proj, w_pe_gate, g_final, packed):
    bsz, s_len, d = x.shape
    n = bsz * s_len
    cap = EC_FACTOR * n // N_EXPERTS
    bias_a, bias_b = rel_bias[:, :H_A], rel_bias[:, H_A:]
    x2d = x.reshape(n, d)
    for l in range(DEPTH):
        w_cat, w_vct, wbd, bup, ng, wbr, wr, br = packed[l]
        lam_init = 0.8 - 0.6 * math.exp(-0.3 * l)
        *outs, vct = _inproj(x2d, g_mix[l][None, :], w_cat, w_vct)
        qa, ka, va, qb, kvb, qc, kc, vc, ac, rc = (o.reshape(bsz, s_len, o.shape[-1]) for o in outs)
        oa = _diffattn(qa, ka, va, lam_a[l], subln_a[l][None, :], bias_a, lam_init)
        ob = _winattn(qb, kvb, sink_b[l], bias_b)
        oc = _gla(qc, kc, vc, vct, ac, rc, wbd, bup, ng)
        z, aff = _merge(x2d, oa.reshape(n, -1), ob.reshape(n, -1), oc.reshape(n, -1), g_mix[l][None, :],
                        w_merge_gate[l], wbr, w_out[l], g_ffn[l][None, :], wr, br)
        aff3 = aff[:N_EXPERTS].reshape(N_EXPERTS, n // LANES, LANES)
        idx, gates = _select(aff3, cap)
        z = _moe(idx.reshape(-1), gates.reshape(N_EXPERTS, cap, 1), z, l, w_exp_gate, w_exp_up, w_exp_down, cap)
        x2d = _pe(z, pe.reshape(DEPTH, n, -1), l, w_pe_gate[l], w_pe_proj[l], g_final[None, :], l == DEPTH - 1)
    return x2d.reshape(bsz, s_len, d)


def kernel(x_prompt, x_sample, p_prompt, p_sample, rel_bias, g_mix, w_in, lam_a, subln_a, sink_b, w_alpha_up,
           b_alpha_up, norm_c, w_branch, w_merge_gate, w_out, g_ffn, w_router, b_router, w_exp_gate, w_exp_up,
           w_exp_down, w_pe_proj, w_pe_gate, g_final):
    packed = [_pack_weights(l, w_in, w_alpha_up, b_alpha_up, norm_c, w_branch, w_router, b_router)
              for l in range(DEPTH)]
    shared = (rel_bias, g_mix, lam_a, subln_a, sink_b, w_merge_gate.astype(BF16), w_out.astype(BF16), g_ffn,
              w_exp_gate.astype(BF16), w_exp_up.astype(BF16), w_exp_down.astype(BF16), w_pe_proj.astype(BF16),
              w_pe_gate.astype(BF16), g_final, packed)
    y_prompt = _trunk(x_prompt, p_prompt, *shared)
    y_sample = _trunk(x_sample, p_sample, *shared)
    return (y_prompt, y_sample)
```

```python
import functools
import math

import jax
import jax.numpy as jnp
from jax import lax
from jax.experimental import pallas as pl
from jax.experimental.pallas import tpu as pltpu

F32 = jnp.float32
BF16 = jnp.bfloat16

D_MODEL = 1024
DEPTH = 2
H_A, DK_A, DV_A = 4, 64, 128
H_B, KV_B, GQ_B, DH_B = 8, 2, 4, 64
WINDOW, BLOCK_B = 128, 128
H_C, DK_C, DV_C = 4, 64, 128
GATE_RANK, GATE_TAU, CHUNK = 16, 16, 64
BRANCH_W, N_BRANCH = 512, 3
N_BUCKETS, MAX_DIST = 32, 128
N_EXPERTS, EC_FACTOR, D_FF = 16, 2, 2048
P_DIM = 256
NORM_EPS = 1e-6
LOG2E = math.log2(math.e)
LANES = 128
ROWS_PER_TILE = 8
VMEM_LIMIT = 56 * 1024 * 1024
MOE_TILE = 512

_IN_COLS = (
    ("qa", 512, BF16, DK_A ** -0.5 * LOG2E), ("ka", 512, BF16, 1.0), ("va", 512, BF16, 1.0),
    ("qb", 1024, BF16, DH_B ** -0.5 * LOG2E), ("kvb", 256, BF16, 1.0),
    ("qc", 512, F32, DK_C ** -0.5), ("kc", 512, F32, 1.0), ("vc", 512, BF16, 1.0),
    ("ac", 128, F32, 1.0), ("rc", 512, BF16, 1.0),
)


def _cparams(n_axes):
    return pltpu.CompilerParams(dimension_semantics=("arbitrary",) * n_axes,
                                vmem_limit_bytes=VMEM_LIMIT)


def _const_spec(shape):
    nd = len(shape)
    return pl.BlockSpec(shape, lambda *_: (0,) * nd, pipeline_mode=pl.Buffered(1))


def _pick(n, pref):
    t = min(n, pref)
    while n % t:
        t //= 2
    return t


def _rms(x):
    return x * lax.rsqrt(jnp.mean(x * x, axis=-1, keepdims=True) + NORM_EPS)


def _t5_bucket(rel):
    half = N_BUCKETS // 2
    max_exact = half // 2
    ret = jnp.where(rel > 0, half, 0)
    n = jnp.abs(rel)
    nf = jnp.maximum(n, 1).astype(F32)
    large = max_exact + (jnp.log(nf / max_exact) / math.log(MAX_DIST / max_exact)
                         * (half - max_exact)).astype(jnp.int32)
    large = jnp.minimum(large, half - 1)
    return ret + jnp.where(n < max_exact, n, large)


def _bias_lookup(table, rel):
    onehot = (_t5_bucket(rel)[..., None] == jnp.arange(N_BUCKETS)).astype(F32)
    return jnp.einsum("...b,bh->...h", onehot, table, precision=lax.Precision.HIGHEST)


def _inproj_kernel(x_ref, g_ref, w_ref, wvt_ref, *out_refs):
    h = (_rms(x_ref[...]) * g_ref[...]).astype(BF16)
    c0 = 0
    for o_ref, (_, width, _, scale) in zip(out_refs, _IN_COLS):
        acc = jnp.dot(h, w_ref[:, c0:c0 + width], preferred_element_type=F32)
        if scale != 1.0:
            acc = acc * scale
        o_ref[...] = acc.astype(o_ref.dtype)
        c0 += width
    vt = lax.dot_general(wvt_ref[...], h, (((1,), (1,)), ((), ())), preferred_element_type=F32)
    out_refs[-1][...] = vt.astype(out_refs[-1].dtype)


def _inproj(x2d, g, w_cat, w_vct):
    n, d = x2d.shape
    tm = _pick(n, 512)
    out_shape = [jax.ShapeDtypeStruct((n, w), dt) for _, w, dt, _ in _IN_COLS]
    out_specs = [pl.BlockSpec((tm, w), lambda i: (i, 0)) for _, w, _, _ in _IN_COLS]
    out_shape.append(jax.ShapeDtypeStruct((w_vct.shape[0], n), BF16))
    out_specs.append(pl.BlockSpec((w_vct.shape[0], tm), lambda i: (0, i)))
    return pl.pallas_call(
        _inproj_kernel,
        out_shape=out_shape,
        grid=(n // tm,),
        in_specs=[pl.BlockSpec((tm, d), lambda i: (i, 0)), _const_spec(g.shape), _const_spec(w_cat.shape),
                  _const_spec(w_vct.shape)],
        out_specs=out_specs,
        compiler_params=_cparams(1),
        name="inproj",
    )(x2d, g, w_cat, w_vct)


def _diffattn_kernel(lam_ref, subg_ref, cst_ref, band_ref, q_ref, k_ref, v_ref, o_ref, s_ref, *, lam_init):
    i = pl.program_id(2)
    nq = pl.num_programs(2)
    tq = q_ref.shape[0]
    s_len = k_ref.shape[0]
    lp = lam_ref[...]
    lam = (jnp.exp(jnp.sum(lp[0:1] * lp[1:2], axis=-1, keepdims=True))
           - jnp.exp(jnp.sum(lp[2:3] * lp[3:4], axis=-1, keepdims=True))) + lam_init
    q = q_ref[...]
    k = k_ref[...]
    lane = lax.broadcasted_iota(jnp.int32, (1, LANES), 1)
    zero = jnp.zeros_like(q)
    col = lax.broadcasted_iota(jnp.int32, (1, s_len), 1)
    base = jnp.where(col < i * tq, cst_ref[0:1, 0:1], cst_ref[1:2, 0:1])
    for m in range(2):
        qm = jnp.where((lane < DK_A) == (m == 0), q, zero)
        s_ref[m] = lax.dot_general(qm, k, (((1,), (1,)), ((), ())), preferred_element_type=F32) + base
    for d in (-1, 0, 1):
        kb = i + d

        @pl.when((kb >= 0) & (kb < nq))
        def _(d=d, kb=kb):
            off = pl.multiple_of(kb * tq, tq)
            delta = band_ref[d + 1]
            for m in range(2):
                s_ref[m, :, pl.ds(off, tq)] += delta

    probs, den = [], []
    for m in range(2):
        s = s_ref[m]
        p = jnp.exp2(s - jnp.max(s, axis=-1, keepdims=True))
        probs.append(p)
        den.append(jnp.sum(p, axis=-1, keepdims=True))
    w = (probs[0] - probs[1] * (lam * den[0] / den[1])).astype(BF16)
    o = jnp.dot(w, v_ref[...], preferred_element_type=F32) * (1.0 / den[0])
    o_ref[...] = (_rms(o) * subg_ref[...] * (1.0 - lam_init)).astype(o_ref.dtype)


def _diffattn(qa, ka, va, lam_l, subg_l, bias_a, lam_init):
    bsz, s_len, _ = qa.shape
    tq = _pick(s_len, 256)
    nq = s_len // tq
    ii = jnp.arange(tq)
    rel = jnp.stack([d * tq + ii[None, :] - ii[:, None] for d in (-1, 0, 1)])
    bias_a = bias_a * LOG2E
    table = _bias_lookup(bias_a, rel)
    c_left = bias_a[N_BUCKETS // 2 - 1]
    c_right = bias_a[N_BUCKETS - 1]
    side = jnp.stack([c_left, c_right, c_right])
    band = (table - side[:, None, None, :]).transpose(3, 0, 1, 2)
    cst = jnp.broadcast_to(jnp.stack([c_left, c_right], axis=1)[:, :, None], (H_A, 2, LANES))
    cst = jnp.pad(cst, ((0, 0), (0, 6), (0, 0)))
    return pl.pallas_call(
        functools.partial(_diffattn_kernel, lam_init=lam_init),
        out_shape=jax.ShapeDtypeStruct((bsz, s_len, H_A * DV_A), BF16),
        grid=(bsz, H_A, nq),
        in_specs=[
            _const_spec(lam_l.shape),
            _const_spec(subg_l.shape),
            pl.BlockSpec((None, 8, LANES), lambda b, h, i: (h, 0, 0)),
            pl.BlockSpec((None, 3, tq, tq), lambda b, h, i: (h, 0, 0, 0)),
            pl.BlockSpec((None, tq, LANES), lambda b, h, i: (b, i, h)),
            pl.BlockSpec((None, s_len, LANES), lambda b, h, i: (b, 0, h)),
            pl.BlockSpec((None, s_len, LANES), lambda b, h, i: (b, 0, h)),
        ],
        out_specs=pl.BlockSpec((None, tq, LANES), lambda b, h, i: (b, i, h)),
        scratch_shapes=[pltpu.VMEM((2, tq, s_len), F32)],
        compiler_params=_cparams(3),
        name="diffattn",
    )(lam_l, subg_l, cst, band, qa, ka, va)


def _winattn_kernel(sink_ref, band_ref, q_ref, kp_ref, kc_ref, kn_ref, o_ref, *, s_len):
    i = pl.program_id(1)
    tq = q_ref.shape[0]
    lane = lax.broadcasted_iota(jnp.int32, (1, LANES), 1)
    row = lax.broadcasted_iota(jnp.int32, (GQ_B * tq, 1), 0)

    def tile(edge):
        kv = jnp.concatenate([kp_ref[...], kc_ref[...], kn_ref[...]], axis=0)
        kk = kv[:, :LANES]
        vv = kv[:, LANES:]
        outs = []
        for c in range(KV_B):
            qs = jnp.concatenate([q_ref[:, h * LANES:(h + 1) * LANES] for h in range(c * GQ_B, (c + 1) * GQ_B)],
                                 axis=0)
            s = lax.dot_general(qs, kk, (((1,), (1,)), ((), ())), preferred_element_type=F32)
            s = s + band_ref[c]
            if edge:
                kabs = i * tq - BLOCK_B + lax.broadcasted_iota(jnp.int32, (1, tq + 2 * BLOCK_B), 1)
                s = jnp.where((kabs >= 0) & (kabs < s_len), s, -jnp.inf)
            sk = sink_ref[c * GQ_B:c * GQ_B + 1, 0:1]
            for g in range(1, GQ_B):
                sk = jnp.where(row >= g * tq, sink_ref[c * GQ_B + g:c * GQ_B + g + 1, 0:1], sk)
            m = jnp.maximum(jnp.max(s, axis=-1, keepdims=True), sk)
            p = jnp.exp2(s - m)
            inv = 1.0 / (jnp.sum(p, axis=-1, keepdims=True) + jnp.exp2(sk - m))
            outs.append(jnp.dot(p.astype(BF16), vv, preferred_element_type=F32) * inv)
        for g in range(GQ_B):
            o_ref[:, g * LANES:(g + 1) * LANES] = jnp.where(
                lane < DH_B, outs[0][g * tq:(g + 1) * tq], outs[1][g * tq:(g + 1) * tq]).astype(o_ref.dtype)

    is_edge = (i == 0) | (i == pl.num_programs(1) - 1)
    pl.when(is_edge)(lambda: tile(True))
    pl.when(jnp.logical_not(is_edge))(lambda: tile(False))


def _winattn(qb, kvb, sink_l, bias_b):
    bsz, s_len, _ = qb.shape
    nb = s_len // BLOCK_B
    tq = _pick(s_len, 512)
    r = tq // BLOCK_B
    rel = jnp.arange(tq + 2 * BLOCK_B)[None, :] - BLOCK_B - jnp.arange(tq)[:, None]
    band = jnp.where((jnp.abs(rel) <= WINDOW)[:, :, None], _bias_lookup(bias_b * LOG2E, rel), -jnp.inf)
    band = band.transpose(2, 0, 1).reshape(KV_B, GQ_B * tq, tq + 2 * BLOCK_B)
    sink = jnp.broadcast_to(sink_l[:, None] * LOG2E, (H_B, LANES))
    kvw = 2 * KV_B * DH_B
    return pl.pallas_call(
        functools.partial(_winattn_kernel, s_len=s_len),
        out_shape=jax.ShapeDtypeStruct((bsz, s_len, H_B * DH_B), BF16),
        grid=(bsz, s_len // tq),
        in_specs=[
            _const_spec(sink.shape),
            _const_spec(band.shape),
            pl.BlockSpec((None, tq, H_B * LANES), lambda b, i: (b, i, 0)),
            pl.BlockSpec((None, BLOCK_B, kvw), lambda b, i: (b, jnp.maximum(i * r - 1, 0), 0)),
            pl.BlockSpec((None, tq, kvw), lambda b, i: (b, i, 0)),
            pl.BlockSpec((None, BLOCK_B, kvw), lambda b, i: (b, jnp.minimum((i + 1) * r, nb - 1), 0)),
        ],
        out_specs=pl.BlockSpec((None, tq, H_B * DH_B), lambda b, i: (b, i, 0)),
        compiler_params=_cparams(2),
        name="winattn",
    )(sink, band, qb, kvb, kvb, kvb)


def _gla_kernel(q_ref, k_ref, v_ref, vt_ref, a_ref, r_ref, wbd_ref, bup_ref, ng_ref, o_ref,
                g_scr, acc_scr, qd_scr, kv_scr, dec_scr, tot_scr, a_scr):
    s_len = q_ref.shape[0]
    nc = s_len // CHUNK
    pair = 2 * CHUNK
    pre = jnp.dot(a_ref[...].astype(BF16), wbd_ref[...], preferred_element_type=F32) + bup_ref[...]
    g_scr[...] = (jnp.minimum(pre, 0.0) - jnp.log1p(jnp.exp(-jnp.abs(pre)))) * (1.0 / GATE_TAU)
    lane = lax.broadcasted_iota(jnp.int32, (1, LANES), 1)
    fwd_lane = lane < DK_C
    first = lax.broadcasted_iota(jnp.int32, (pair, 1), 0) < CHUNK
    ri = lax.broadcasted_iota(jnp.int32, (2 * pair, pair), 0)
    ci = lax.broadcasted_iota(jnp.int32, (2 * pair, pair), 1)
    sh = CHUNK.bit_length() - 1
    blk, rr, cc = ri >> sh, ri & (CHUNK - 1), ci & (CHUNK - 1)
    same = (ci >> sh) == (blk & 1)
    cum = jnp.where(same & (((blk < 2) & (cc <= rr)) | ((blk >= 2) & (cc >= rr))), 1.0, 0.0).astype(BF16)
    qrow = lax.broadcasted_iota(jnp.int32, (pair, LANES), 0) & (CHUNK - 1)
    kcol = lax.broadcasted_iota(jnp.int32, (pair, LANES), 1)
    keep = ((kcol < CHUNK) & (kcol <= qrow)) | ((kcol >= CHUNK) & (kcol - CHUNK >= qrow))
    nt = (((1,), (1,)), ((), ()))

    def cumsum(p, carry):
        off = pl.multiple_of(p * pair, pair)
        g = g_scr[pl.ds(off, pair), :]
        g1 = g.astype(BF16)
        r1 = g - g1.astype(F32)
        g2 = r1.astype(BF16)
        g3 = (r1 - g2.astype(F32)).astype(BF16)
        b3 = jnp.dot(cum, jnp.concatenate([g1, g2, g3], axis=1), preferred_element_type=F32)
        bb = b3[:, :LANES] + b3[:, LANES:2 * LANES] + b3[:, 2 * LANES:]
        g_scr[pl.ds(off, pair), :] = jnp.where(fwd_lane, bb[:pair], bb[pair:])
        tot_scr[2 * p] = jnp.where(fwd_lane, bb[CHUNK - 1:CHUNK], bb[pair:pair + 1])
        tot_scr[2 * p + 1] = jnp.where(fwd_lane, bb[pair - 1:pair], bb[pair + CHUNK:pair + CHUNK + 1])
        return carry

    lax.fori_loop(0, nc // 2, cumsum, 0, unroll=4)

    def phase1(p, carry):
        off = pl.multiple_of(p * pair, pair)
        b = g_scr[pl.ds(off, pair), :]
        t0 = tot_scr[2 * p]
        t1 = tot_scr[2 * p + 1]
        q = q_ref[pl.ds(off, pair), :]
        k = k_ref[pl.ds(off, pair), :]
        qd = q * jnp.exp(b)
        kd = k * jnp.exp(-b)
        ki = (k * jnp.exp(jnp.where(first, t0, t1) - b)).astype(BF16)
        qd_f = jnp.where(fwd_lane, qd, 0.0).astype(BF16)
        qd_b = jnp.where(fwd_lane, 0.0, qd).astype(BF16)
        qd_scr[2 * p] = jnp.concatenate([qd_f[:CHUNK], qd_b[:CHUNK]], axis=0)
        qd_scr[2 * p + 1] = jnp.concatenate([qd_f[CHUNK:], qd_b[CHUNK:]], axis=0)
        kd_f = jnp.where(fwd_lane, kd, 0.0).astype(BF16)
        kd_b = jnp.where(fwd_lane, 0.0, kd).astype(BF16)
        kd4 = jnp.concatenate([kd_f[:CHUNK], kd_b[:CHUNK], kd_f[CHUNK:], kd_b[CHUNK:]], axis=0)
        sc = lax.dot_general(qd.astype(BF16), kd4, nt, preferred_element_type=F32)
        a = jnp.where(keep, jnp.where(first, sc[:, :LANES], sc[:, LANES:]), 0.0).astype(BF16)
        zero = jnp.zeros_like(a)
        a_scr[p] = jnp.concatenate([jnp.where(first, a, zero), jnp.where(first, zero, a)], axis=1)
        vt = vt_ref[:, pl.ds(off, pair)]
        vzero = jnp.zeros_like(vt)
        kv_scr[2 * p] = jnp.dot(jnp.where(fwd_lane, vt, vzero), ki, preferred_element_type=F32)
        kv_scr[2 * p + 1] = jnp.dot(jnp.where(fwd_lane, vzero, vt), ki, preferred_element_type=F32)
        dec_scr[2 * p] = jnp.exp(t0)
        dec_scr[2 * p + 1] = jnp.exp(t1)
        return carry

    lax.fori_loop(0, nc // 2, phase1, 0, unroll=8)

    def intra(p, carry):
        off = pl.multiple_of(p * pair, pair)
        v = v_ref[pl.ds(off, pair), :]
        v4 = jnp.concatenate([v[:CHUNK], v[:CHUNK], v[CHUNK:], v[CHUNK:]], axis=0)
        acc_scr[pl.ds(off, pair), :] = jnp.dot(a_scr[p], v4, preferred_element_type=F32)
        return carry

    lax.fori_loop(0, nc // 2, intra, 0, unroll=8)

    def phase2(t, state):
        cf = t
        cb = nc - 1 - t
        lhs = jnp.concatenate([qd_scr[cf, 0:CHUNK, :], qd_scr[cb, CHUNK:2 * CHUNK, :]], axis=0)
        o2 = lax.dot_general(lhs, state.astype(BF16), nt, preferred_element_type=F32)
        acc_scr[pl.ds(pl.multiple_of(cf * CHUNK, CHUNK), CHUNK), :] += o2[:CHUNK]
        acc_scr[pl.ds(pl.multiple_of(cb * CHUNK, CHUNK), CHUNK), :] += o2[CHUNK:]
        dec = jnp.where(fwd_lane, dec_scr[cf], dec_scr[cb])
        inc = jnp.where(fwd_lane, kv_scr[cf], kv_scr[cb])
        return dec * state + inc

    lax.fori_loop(0, nc, phase2, jnp.zeros((DV_C, LANES), F32), unroll=8)
    r = r_ref[...].astype(F32)
    o_ref[...] = (_rms(acc_scr[...]) * ng_ref[...] * (r * jax.nn.sigmoid(r))).astype(o_ref.dtype)


def _gla(qc, kc, vc, vct, ac, rc, wbd, bup, ng):
    bsz, s_len, _ = qc.shape
    nc = s_len // CHUNK
    blk = lambda: pl.BlockSpec((None, s_len, LANES), lambda b, h: (b, 0, h))
    per_head = lambda r: pl.BlockSpec((None, r, LANES), lambda b, h: (h, 0, 0))
    return pl.pallas_call(
        _gla_kernel,
        out_shape=jax.ShapeDtypeStruct((bsz, s_len, H_C * DV_C), BF16),
        grid=(bsz, H_C),
        in_specs=[blk(), blk(), blk(),
                  pl.BlockSpec((DV_C, s_len), lambda b, h: (h, b)),
                  pl.BlockSpec((None, s_len, LANES), lambda b, h: (b, 0, 0)),
                  blk(), per_head(LANES), per_head(1), per_head(1)],
        out_specs=blk(),
        scratch_shapes=[pltpu.VMEM((s_len, LANES), F32), pltpu.VMEM((s_len, LANES), F32),
                        pltpu.VMEM((nc, 2 * CHUNK, LANES), BF16), pltpu.VMEM((nc, DV_C, LANES), F32),
                        pltpu.VMEM((nc, 1, LANES), F32), pltpu.VMEM((nc, 1, LANES), F32),
                        pltpu.VMEM((nc // 2, 2 * CHUNK, 2 * LANES), BF16)],
        compiler_params=_cparams(2),
        name="gla",
    )(qc, kc, vc, vct, ac, rc, wbd, bup, ng)


def _merge_kernel(x_ref, oa_ref, ob_ref, oc_ref, gmix_ref, wmg_ref, wbr_ref, wout_ref, gffn_ref, wr_ref, br_ref,
                  z_ref, aff_ref):
    x = x_ref[...]
    d = x.shape[-1]
    h = (_rms(x) * gmix_ref[...]).astype(BF16)
    acc = None
    for g, o_ref in enumerate((oa_ref, ob_ref, oc_ref)):
        gate = jax.nn.sigmoid(jnp.dot(h, wmg_ref[:, g * D_MODEL:(g + 1) * D_MODEL], preferred_element_type=F32))
        br = jnp.dot(o_ref[...], wbr_ref[g], preferred_element_type=F32)
        acc = gate * br if acc is None else acc + gate * br
    x1 = x + jnp.dot(acc.astype(BF16), wout_ref[...], preferred_element_type=F32)
    xt = _rms(x1) * gffn_ref[...]
    z_ref[:, :d] = x1
    z_ref[:, d:] = xt
    logits = jnp.dot(xt.astype(BF16), wr_ref[...], preferred_element_type=F32) + br_ref[...]
    lane = lax.broadcasted_iota(jnp.int32, logits.shape, 1)
    logits = jnp.where(lane < N_EXPERTS, logits, -jnp.inf)
    e = jnp.exp(logits - jnp.max(logits, axis=-1, keepdims=True))
    aff_ref[...] = (e / jnp.sum(e, axis=-1, keepdims=True)).T


def _merge(x2d, oa, ob, oc, gmix, wmg, wbr, wout, gffn, wr, br):
    n, d = x2d.shape
    tm = _pick(n, 512)
    row = lambda w: pl.BlockSpec((tm, w), lambda i: (i, 0))
    return pl.pallas_call(
        _merge_kernel,
        out_shape=[jax.ShapeDtypeStruct((n, 2 * d), F32), jax.ShapeDtypeStruct((LANES, n), F32)],
        grid=(n // tm,),
        in_specs=[row(d), row(BRANCH_W), row(BRANCH_W), row(BRANCH_W)]
        + [_const_spec(a.shape) for a in (gmix, wmg, wbr, wout, gffn, wr, br)],
        out_specs=[row(2 * d), pl.BlockSpec((LANES, tm), lambda i: (0, i))],
        compiler_params=_cparams(1),
        name="merge",
    )(x2d, oa, ob, oc, gmix, wmg, wbr, wout, gffn, wr, br)


def _prefix_counts(m):
    r = m.shape[0]
    li = lax.broadcasted_iota(jnp.int32, (LANES, LANES), 0)
    lj = lax.broadcasted_iota(jnp.int32, (LANES, LANES), 1)
    within = jnp.dot(m.astype(BF16), (li <= lj).astype(BF16), preferred_element_type=F32)
    rowtot = jnp.broadcast_to(within[:, LANES - 1:LANES], (r, LANES))
    ri = lax.broadcasted_iota(jnp.int32, (r, r), 0)
    rj = lax.broadcasted_iota(jnp.int32, (r, r), 1)
    before = jnp.dot((rj < ri).astype(F32), rowtot, preferred_element_type=F32, precision=lax.Precision.HIGHEST)
    return within, rowtot, before


def _select_kernel(aff_all_ref, aff_ref, idx_ref, gate_ref, thr_scr, *, cap):
    ex = pl.program_id(0)

    @pl.when(ex == 0)
    def _():
        bits_all = pltpu.bitcast(aff_all_ref[...], jnp.int32)

        def bit_step(t, thr):
            cand = thr | jnp.left_shift(jnp.int32(1), 30 - t)
            hit = (bits_all >= cand).astype(jnp.int32)
            cnt = jnp.sum(jnp.sum(hit, axis=1, keepdims=True), axis=2, keepdims=True)
            return jnp.where(cnt >= cap, cand, thr)

        thr_all = lax.fori_loop(0, 31, bit_step, jnp.zeros((aff_all_ref.shape[0], 1, 1), jnp.int32))
        thr_scr[...] = jnp.broadcast_to(thr_all, thr_scr.shape)

    aff = aff_ref[...]
    r = aff.shape[0]
    bits = pltpu.bitcast(aff, jnp.int32)
    thr = thr_scr[ex][:, 0:1]
    gt = bits > thr
    eq = bits == thr
    need = (cap - jnp.sum(gt.astype(jnp.int32), keepdims=True)).astype(F32)
    eqf = eq.astype(F32)
    w_eq, _, b_eq = _prefix_counts(eqf)
    sel = gt | (eq & ((w_eq - eqf + b_eq) < need))
    within, rowtot, before = _prefix_counts(sel.astype(F32))
    row_incl = before[:, 0:1] + rowtot[:, 0:1]
    slot = lax.broadcasted_iota(jnp.int32, (1, cap), 1).astype(F32)
    done = row_incl <= slot
    r_j = jnp.sum(done.astype(F32), axis=0, keepdims=True)
    off_j = jnp.sum(jnp.where(done, rowtot[:, 0:1], 0.0), axis=0, keepdims=True)
    onehot = lax.broadcasted_iota(jnp.int32, (r, cap), 0).astype(F32) == r_j
    cnt_t = jnp.dot(within.T.astype(BF16), onehot.astype(BF16), preferred_element_type=F32)
    lane_j = jnp.sum((cnt_t <= (slot - off_j)).astype(F32), axis=0, keepdims=True)
    idx_ref[...] = (r_j * LANES + lane_j).astype(jnp.int32)
    aff_t = jnp.dot(aff.T, onehot.astype(F32), preferred_element_type=F32, precision=lax.Precision.HIGHEST)
    lsub = lax.broadcasted_iota(jnp.int32, (LANES, cap), 0).astype(F32)
    gate_ref[...] = jnp.sum(jnp.where(lsub == lane_j, aff_t, 0.0), axis=0, keepdims=True)


def _select(aff3, cap):
    e, r, _ = aff3.shape
    return pl.pallas_call(
        functools.partial(_select_kernel, cap=cap),
        out_shape=[jax.ShapeDtypeStruct((e, 1, cap), jnp.int32), jax.ShapeDtypeStruct((e, 1, cap), F32)],
        grid=(e,),
        in_specs=[_const_spec(aff3.shape), pl.BlockSpec((None, r, LANES), lambda i: (i, 0, 0))],
        out_specs=[pl.BlockSpec((None, 1, cap), lambda i: (i, 0, 0)), pl.BlockSpec((None, 1, cap), lambda i: (i, 0, 0))],
        scratch_shapes=[pltpu.VMEM((e, 1, LANES), jnp.int32)],
        compiler_params=_cparams(1),
        name="select",
    )(aff3, aff3)


def _moe_kernel(idx_ref, gate_ref, z_in_hbm, wg_ref, wu_ref, wd_ref, z_hbm, zbuf, obuf, gsem, ssem, *,
                tile, ff_chunk, nt):
    del z_in_hbm
    e = pl.program_id(0)
    s = pl.program_id(1)
    d = obuf.shape[-1]
    slot = s % 2
    n_ff = D_FF // ff_chunk

    groups = tile // ROWS_PER_TILE
    sh = ROWS_PER_TILE.bit_length() - 1

    def issue(step, buf_slot, gather):
        base = (e * nt + step) * tile

        def body(g, carry):
            for k in range(ROWS_PER_TILE):
                t = idx_ref[base + g * ROWS_PER_TILE + k]
                hi, lo = t >> sh, t & (ROWS_PER_TILE - 1)
                if gather:
                    pltpu.make_async_copy(z_hbm.at[hi, pl.ds(lo, 1)], zbuf.at[buf_slot, g, pl.ds(k, 1)],
                                          gsem.at[buf_slot]).start()
                else:
                    pltpu.make_async_copy(obuf.at[buf_slot, g, pl.ds(k, 1)], z_hbm.at[hi, pl.ds(lo, 1), pl.ds(0, d)],
                                          ssem.at[buf_slot]).start()
            return carry

        lax.fori_loop(0, groups, body, 0)

    def wait_gather(src_slot):
        pltpu.make_async_copy(z_hbm.at[pl.ds(0, groups)], zbuf.at[src_slot], gsem.at[src_slot]).wait()

    def wait_scatter(src_slot):
        pltpu.make_async_copy(obuf.at[src_slot], z_hbm.at[pl.ds(0, groups), :, pl.ds(0, d)], ssem.at[src_slot]).wait()

    @pl.when(s == 0)
    def _():
        issue(s, slot, True)

    wait_gather(slot)

    @pl.when(s + 1 < nt)
    def _():
        issue(s + 1, 1 - slot, True)

    @pl.when(s >= 2)
    def _():
        wait_scatter(slot)

    zt = zbuf[slot].reshape(tile, 2 * d)
    x = zt[:, d:].astype(BF16)
    ye = None
    for c0 in range(0, D_FF, ff_chunk):
        hg = jnp.dot(x, wg_ref[:, c0:c0 + ff_chunk], preferred_element_type=F32)
        hu = jnp.dot(x, wu_ref[:, c0:c0 + ff_chunk], preferred_element_type=F32)
        hid = (hg * jax.nn.sigmoid(hg) * hu).astype(BF16)
        part = jnp.dot(hid, wd_ref[c0:c0 + ff_chunk, :], preferred_element_type=F32)
        ye = part if ye is None else ye + part
    obuf[slot] = (zt[:, :d] + ye * gate_ref[...]).reshape(groups, ROWS_PER_TILE, d)
    issue(s, slot, False)

    @pl.when(s == nt - 1)
    def _():
        if nt > 1:
            wait_scatter(1 - slot)
        wait_scatter(slot)


def _moe(idx_flat, gate_col, z, layer, wg, wu, wd, cap):
    n, d2 = z.shape
    d = d2 // 2
    tile = _pick(cap, MOE_TILE)
    grid_spec = pltpu.PrefetchScalarGridSpec(
        num_scalar_prefetch=1,
        grid=(N_EXPERTS, cap // tile),
        in_specs=[
            pl.BlockSpec((None, tile, 1), lambda e, s, idx: (e, s, 0)),
            pl.BlockSpec(memory_space=pl.ANY),
            pl.BlockSpec((None, None, d, D_FF), lambda e, s, idx: (layer, e, 0, 0)),
            pl.BlockSpec((None, None, d, D_FF), lambda e, s, idx: (layer, e, 0, 0)),
            pl.BlockSpec((None, None, D_FF, d), lambda e, s, idx: (layer, e, 0, 0)),
        ],
        out_specs=pl.BlockSpec(memory_space=pl.ANY),
        scratch_shapes=[pltpu.VMEM((2, tile // ROWS_PER_TILE, ROWS_PER_TILE, d2), F32),
                        pltpu.VMEM((2, tile // ROWS_PER_TILE, ROWS_PER_TILE, d), F32),
                        pltpu.SemaphoreType.DMA((2,)), pltpu.SemaphoreType.DMA((2,))],
    )
    z3 = z.reshape(n // ROWS_PER_TILE, ROWS_PER_TILE, d2)
    return pl.pallas_call(
        functools.partial(_moe_kernel, tile=tile, ff_chunk=512, nt=cap // tile),
        out_shape=jax.ShapeDtypeStruct(z3.shape, F32),
        grid_spec=grid_spec,
        input_output_aliases={2: 0},
        compiler_params=_cparams(2),
        name="moe",
    )(idx_flat, gate_col, z3, wg, wu, wd).reshape(n, d2)


def _pe_kernel(x_ref, pe_ref, wpg_ref, wpp_ref, gfin_ref, o_ref, *, last):
    x = x_ref[...]
    gate = jax.nn.sigmoid(jnp.dot(x.astype(BF16), wpg_ref[...], preferred_element_type=F32))
    x = x + gate * jnp.dot(pe_ref[...].astype(BF16), wpp_ref[...], preferred_element_type=F32)
    if last:
        x = _rms(x) * gfin_ref[...]
    o_ref[...] = x


def _pe(z, pe3d, layer, wpg, wpp, gfin, last):
    n, d = z.shape[0], z.shape[1] // 2
    tm = _pick(n, 512)
    return pl.pallas_call(
        functools.partial(_pe_kernel, last=last),
        out_shape=jax.ShapeDtypeStruct((n, d), F32),
        grid=(n // tm,),
        in_specs=[pl.BlockSpec((tm, d), lambda i: (i, 0)), pl.BlockSpec((None, tm, P_DIM), lambda i: (layer, i, 0)),
                  _const_spec(wpg.shape), _const_spec(wpp.shape), _const_spec(gfin.shape)],
        out_specs=pl.BlockSpec((tm, d), lambda i: (i, 0)),
        compiler_params=_cparams(1),
        name="pe",
    )(z, pe3d, wpg, wpp, gfin)


def _pack_weights(l, w_in, w_alpha_up, b_alpha_up, norm_c, w_branch, w_router, b_router):
    widths = (512, 512, 512, 512, 128, 128, 256, 256, 512, 2 * GATE_RANK, 512)
    offs = [0]
    for w in widths:
        offs.append(offs[-1] + w)
    wqa, wka, wva, wqb, wkb, wvb, wqc, wkc, wvc, wac, wrc = (w_in[l][:, offs[i]:offs[i + 1]] for i in range(11))
    zeros64 = jnp.zeros((D_MODEL, DH_B), F32)
    qb_slots = []
    for h in range(H_B):
        wh = wqb[:, h * DH_B:(h + 1) * DH_B]
        qb_slots += [wh, zeros64] if h < GQ_B else [zeros64, wh]
    dup = lambda w: jnp.concatenate([jnp.concatenate([w[:, h * DK_C:(h + 1) * DK_C]] * 2, axis=1) for h in range(H_C)], axis=1)
    w_cat = jnp.concatenate(
        [wqa, wka, wva, jnp.concatenate(qb_slots, axis=1), wkb, wvb, dup(wqc), dup(wkc), wvc,
         jnp.pad(wac, ((0, 0), (0, LANES - 2 * GATE_RANK))), wrc], axis=1).astype(BF16)
    wbd = jnp.zeros((H_C, LANES, LANES), F32)
    bup = []
    for h in range(H_C):
        sl = slice(h * DK_C, (h + 1) * DK_C)
        wbd = wbd.at[h, 0:GATE_RANK, 0:DK_C].set(w_alpha_up[l, 0][:, sl])
        wbd = wbd.at[h, GATE_RANK:2 * GATE_RANK, DK_C:].set(w_alpha_up[l, 1][:, sl])
        bup.append(jnp.concatenate([b_alpha_up[l, 0, sl], b_alpha_up[l, 1, sl]])[None, :])
    bup = jnp.stack(bup)
    ng = norm_c[l][:, None, :]
    order = [h for g in range(GQ_B) for h in (g, GQ_B + g)]
    wb1 = jnp.concatenate([w_branch[l, 1][h * DH_B:(h + 1) * DH_B] for h in order], axis=0)
    wbr = jnp.stack([w_branch[l, 0], wb1, w_branch[l, 2]]).astype(BF16)
    wr = jnp.pad(w_router[l], ((0, 0), (0, LANES - N_EXPERTS))).astype(BF16)
    br = jnp.pad(b_router[l], (0, LANES - N_EXPERTS))[None, :]
    return w_cat, wvc.T.astype(BF16), wbd.astype(BF16), bup, ng, wbr, wr, br


def _trunk(x, pe, rel_bias, g_mix, lam_a, subln_a, sink_b, w_merge_gate, w_out, g_ffn, w_exp_gate, w_exp_up,
           w_exp_down, w_pe_proj, w_pe_gate, g_final, packed):
    bsz, s_len, d = x.shape
    n = bsz * s_len
    cap = EC_FACTOR * n // N_EXPERTS
    bias_a, bias_b = rel_bias[:, :H_A], rel_bias[:, H_A:]
    x2d = x.reshape(n, d)
    for l in range(DEPTH):
        w_cat, w_vct, wbd, bup, ng, wbr, wr, br = packed[l]
        lam_init = 0.8 - 0.6 * math.exp(-0.3 * l)
        *outs, vct = _inproj(x2d, g_mix[l][None, :], w_cat, w_vct)
        qa, ka, va, qb, kvb, qc, kc, vc, ac, rc = (o.reshape(bsz, s_len, o.shape[-1]) for o in outs)
        oa = _diffattn(qa, ka, va, lam_a[l], subln_a[l][None, :], bias_a, lam_init)
        ob = _winattn(qb, kvb, sink_b[l], bias_b)
        oc = _gla(qc, kc, vc, vct, ac, rc, wbd, bup, ng)
        z, aff = _merge(x2d, oa.reshape(n, -1), ob.reshape(n, -1), oc.reshape(n, -1), g_mix[l][None, :],
                        w_merge_gate[l], wbr, w_out[l], g_ffn[l][None, :], wr, br)
        aff3 = aff[:N_EXPERTS].reshape(N_EXPERTS, n // LANES, LANES)
        idx, gates = _select(aff3, cap)
        z = _moe(idx.reshape(-1), gates.reshape(N_EXPERTS, cap, 1), z, l, w_exp_gate, w_exp_up, w_exp_down, cap)
        x2d = _pe(z, pe.reshape(DEPTH, n, -1), l, w_pe_gate[l], w_pe_proj[l], g_final[None, :], l == DEPTH - 1)
    return x2d.reshape(bsz, s_len, d)


def kernel(x_prompt, x_sample, p_prompt, p_sample, rel_bias, g_mix, w_in, lam_a, subln_a, sink_b, w_alpha_up,
           b_alpha_up, norm_c, w_branch, w_merge_gate, w_out, g_ffn, w_router, b_router, w_exp_gate, w_exp_up,
           w_exp_down, w_pe_proj, w_pe_gate, g_final):
    packed = [_pack_weights(l, w_in, w_alpha_up, b_alpha_up, norm_c, w_branch, w_router, b_router)
              for l in range(DEPTH)]
    shared = (rel_bias, g_mix, lam_a, subln_a, sink_b, w_merge_gate.astype(BF16), w_out.astype(BF16), g_ffn,
              w_exp_gate.astype(BF16), w_exp_up.astype(BF16), w_exp_down.astype(BF16), w_pe_proj.astype(BF16),
              w_pe_gate.astype(BF16), g_final, packed)
    y_prompt = _trunk(x_prompt, p_prompt, *shared)
    y_sample = _trunk(x_sample, p_sample, *shared)
    return (y_prompt, y_sample)
```
